```python
import jax, jax.numpy as jnp
from jax import lax
import numpy as np

D_MODEL = 1024
BATCH = 8
SEQ = 4096
DEPTH = 2

N_EVEN = (DEPTH + 1) // 2
N_ODD = DEPTH // 2
PLE_DIM = 256
LN_EPS = 1e-5
RMS_EPS = 1e-5
DEEPNORM_ALPHA = (2.0 * DEPTH) ** 0.25
DEEPNORM_BETA = (8.0 * DEPTH) ** -0.25

FOX_HEAD_DIM = 64
FOX_HEADS = D_MODEL // FOX_HEAD_DIM
FOX_Q_BLOCK = 128
FOX_IN = 3 * D_MODEL + FOX_HEADS

SSD_EXPAND = 2
SSD_D_INNER = SSD_EXPAND * D_MODEL
SSD_HEAD_DIM = 64
SSD_HEADS = SSD_D_INNER // SSD_HEAD_DIM
SSD_GROUPS = 8
SSD_HEADS_PER_GROUP = SSD_HEADS // SSD_GROUPS
SSD_STATE = 128
SSD_CONV = 4
SSD_CHUNK = 128
SSD_CONV_CH = SSD_D_INNER + 2 * SSD_GROUPS * SSD_STATE
SSD_IN = SSD_D_INNER + SSD_CONV_CH + SSD_HEADS

FFN_DIM = 2816
N_EXPERTS = 8
TOP_K = 2
EXPERT_DIM = 3584

kernel_name = 'fox_ssd_interleaved_deepnorm_moe'


def layer_norm(x, g, b):
    xf = x.astype(jnp.float32)
    mu = jnp.mean(xf, axis=-1, keepdims=True)
    var = jnp.mean(jnp.square(xf - mu), axis=-1, keepdims=True)
    return ((xf - mu) * lax.rsqrt(var + LN_EPS) * g + b).astype(x.dtype)


def forgetting_attention(x, w_in, b_f, w_o):
    bsz, seq, _ = x.shape
    proj = x @ w_in
    q, k, v, f_logit = jnp.split(proj, [D_MODEL, 2 * D_MODEL, 3 * D_MODEL], axis=-1)

    def heads(t):
        return t.reshape(bsz, seq, FOX_HEADS, FOX_HEAD_DIM).transpose(0, 2, 1, 3)

    q, k, v = heads(q), heads(k), heads(v)
    log_f = jax.nn.log_sigmoid((f_logit + b_f).astype(jnp.float32))
    cum = jnp.cumsum(log_f, axis=1).transpose(0, 2, 1)
    scale = FOX_HEAD_DIM ** -0.5
    key_pos = jnp.arange(seq)

    def q_block(start):
        qb = lax.dynamic_slice_in_dim(q, start, FOX_Q_BLOCK, axis=2)
        cb = lax.dynamic_slice_in_dim(cum, start, FOX_Q_BLOCK, axis=2)
        logits = jnp.einsum('bhqd,bhkd->bhqk', qb, k).astype(jnp.float32) * scale
        logits = logits + cb[..., :, None] - cum[..., None, :]
        q_pos = start + jnp.arange(FOX_Q_BLOCK)
        causal = key_pos[None, :] <= q_pos[:, None]
        logits = jnp.where(causal, logits, -jnp.inf)
        probs = jax.nn.softmax(logits, axis=-1).astype(v.dtype)
        return jnp.einsum('bhqk,bhkd->bhqd', probs, v)

    starts = jnp.arange(0, seq, FOX_Q_BLOCK)
    out = lax.map(q_block, starts)
    out = out.transpose(1, 0, 3, 2, 4).reshape(bsz, seq, D_MODEL)
    return out @ w_o


def causal_depthwise_conv(u, w, b):
    out = lax.conv_general_dilated(
        u, w[:, None, :].astype(u.dtype), window_strides=(1,), padding=[(SSD_CONV - 1, 0)],
        dimension_numbers=('NWC', 'WIO', 'NWC'), feature_group_count=u.shape[-1])
    return out + b


def ssd_mixer(x, w_in, conv_w, conv_b, dt_bias, a_log, d_skip, norm_g, w_out):
    bsz, seq, _ = x.shape
    G, R, P, N, L = SSD_GROUPS, SSD_HEADS_PER_GROUP, SSD_HEAD_DIM, SSD_STATE, SSD_CHUNK
    nc = seq // L
    f32 = jnp.float32
    proj = x @ w_in
    z, xbc, dt_raw = jnp.split(proj, [SSD_D_INNER, SSD_D_INNER + SSD_CONV_CH], axis=-1)
    xbc = jax.nn.silu(causal_depthwise_conv(xbc, conv_w, conv_b))
    xs, bm, cm = jnp.split(xbc, [SSD_D_INNER, SSD_D_INNER + G * N], axis=-1)
    xs = xs.astype(f32).reshape(bsz, nc, L, G, R, P)
    bm = bm.astype(f32).reshape(bsz, nc, L, G, N)
    cm = cm.astype(f32).reshape(bsz, nc, L, G, N)
    dt = jax.nn.softplus(dt_raw.astype(f32) + dt_bias.astype(f32)).reshape(bsz, nc, L, G, R)
    a = -jnp.exp(a_log.astype(f32)).reshape(G, R)
    da_cs = jnp.cumsum(dt * a, axis=2)
    xdt = xs * dt[..., None]
    seg = da_cs[:, :, :, None] - da_cs[:, :, None, :]
    causal = jnp.tril(jnp.ones((L, L), dtype=bool))[:, :, None, None]
    decay = jnp.exp(jnp.where(causal, seg, -jnp.inf))
    cb = jnp.einsum('bclgn,bcsgn->bclsg', cm, bm)
    y_diag = jnp.einsum('bclsgr,bcsgrp->bclgrp', cb[..., None] * decay, xdt)
    decay_to_end = jnp.exp(da_cs[:, :, -1:] - da_cs)
    states = jnp.einsum('bclgn,bclgrp->bcgrpn', bm, xdt * decay_to_end[..., None])
    chunk_decay = jnp.exp(da_cs[:, :, -1])

    def step(h, inp):
        s_c, a_c = inp
        return a_c[..., None, None] * h + s_c, h

    h0 = jnp.zeros((bsz, G, R, P, N), f32)
    _, h_in = lax.scan(step, h0, (states.transpose(1, 0, 2, 3, 4, 5), chunk_decay.transpose(1, 0, 2, 3)))
    h_in = h_in.transpose(1, 0, 2, 3, 4, 5)
    y_off = jnp.einsum('bclgn,bcgrpn->bclgrp', cm, h_in) * jnp.exp(da_cs)[..., None]
    y = y_diag + y_off + d_skip.astype(f32).reshape(G, R, 1) * xs
    y = y.reshape(bsz, seq, SSD_D_INNER)
    y = y * jax.nn.silu(z.astype(f32))
    yg = y.reshape(bsz, seq, G, SSD_D_INNER // G)
    yg = yg * lax.rsqrt(jnp.mean(jnp.square(yg), axis=-1, keepdims=True) + RMS_EPS)
    y = (yg.reshape(bsz, seq, SSD_D_INNER) * norm_g).astype(x.dtype)
    return y @ w_out


def swiglu(x, w_gate, w_up, w_down):
    return (jax.nn.silu(x @ w_gate) * (x @ w_up)) @ w_down


def moe_swiglu(x, router, w_gate, w_up, w_down):
    logits = (x @ router).astype(jnp.float32)
    top_vals, top_idx = lax.top_k(logits, TOP_K)
    top_w = jax.nn.softmax(top_vals, axis=-1)
    gates = jnp.sum(jax.nn.one_hot(top_idx, N_EXPERTS, dtype=jnp.float32) * top_w[..., None], axis=-2)
    out = jnp.zeros_like(x)
    for e in range(N_EXPERTS):
        y_e = swiglu(x, w_gate[e], w_up[e], w_down[e])
        out = out + (gates[..., e:e + 1] * y_e).astype(x.dtype)
    return out


def setup_inputs(seed: int = 0) -> dict:
    key = jax.random.key(seed)
    ks = iter(jax.random.split(key, 40))
    f32 = jnp.float32

    def normal(shape, scale):
        return jax.random.normal(next(ks), shape, f32) * scale

    def gain(shape):
        return 1.0 + normal(shape, 0.02)

    beta = DEEPNORM_BETA
    dt0 = jnp.exp(jax.random.uniform(next(ks), (N_ODD, SSD_HEADS), f32, np.log(1e-3), np.log(1e-1)))
    dt_bias = dt0 + jnp.log(-jnp.expm1(-dt0))
    a_log = jnp.log(jax.random.uniform(next(ks), (N_ODD, SSD_HEADS), f32, 1.0, 16.0))
    fox_b_f = jax.random.uniform(next(ks), (N_EVEN, FOX_HEADS), f32, 1.0, 6.0)
    return {
        'x': normal((BATCH, SEQ, D_MODEL), 1.0),
        'p': normal((DEPTH, BATCH, SEQ, PLE_DIM), 1.0),
        'ln_mix_g': gain((DEPTH, D_MODEL)),
        'ln_mix_b': normal((DEPTH, D_MODEL), 0.02),
        'ln_ffn_g': gain((DEPTH, D_MODEL)),
        'ln_ffn_b': normal((DEPTH, D_MODEL), 0.02),
        'fox_w_in': normal((N_EVEN, D_MODEL, FOX_IN), D_MODEL ** -0.5),
        'fox_b_f': fox_b_f,
        'fox_w_o': normal((N_EVEN, D_MODEL, D_MODEL), beta * D_MODEL ** -0.5),
        'ssd_w_in': normal((N_ODD, D_MODEL, SSD_IN), D_MODEL ** -0.5),
        'ssd_conv_w': normal((N_ODD, SSD_CONV, SSD_CONV_CH), SSD_CONV ** -0.5),
        'ssd_conv_b': normal((N_ODD, SSD_CONV_CH), 0.02),
        'ssd_dt_bias': dt_bias,
        'ssd_a_log': a_log,
        'ssd_d': gain((N_ODD, SSD_HEADS)),
        'ssd_norm_g': gain((N_ODD, SSD_D_INNER)),
        'ssd_w_out': normal((N_ODD, SSD_D_INNER, D_MODEL), beta * SSD_D_INNER ** -0.5),
        'ffn_w_gate': normal((N_EVEN, D_MODEL, FFN_DIM), D_MODEL ** -0.5),
        'ffn_w_up': normal((N_EVEN, D_MODEL, FFN_DIM), D_MODEL ** -0.5),
        'ffn_w_down': normal((N_EVEN, FFN_DIM, D_MODEL), beta * FFN_DIM ** -0.5),
        'moe_router': normal((N_ODD, D_MODEL, N_EXPERTS), D_MODEL ** -0.5),
        'moe_w_gate': normal((N_ODD, N_EXPERTS, D_MODEL, EXPERT_DIM), D_MODEL ** -0.5),
        'moe_w_up': normal((N_ODD, N_EXPERTS, D_MODEL, EXPERT_DIM), D_MODEL ** -0.5),
        'moe_w_down': normal((N_ODD, N_EXPERTS, EXPERT_DIM, D_MODEL), beta * EXPERT_DIM ** -0.5),
        'ple_w_proj': normal((DEPTH, PLE_DIM, D_MODEL), PLE_DIM ** -0.5),
        'ple_w_gate': normal((DEPTH, D_MODEL, D_MODEL), D_MODEL ** -0.5),
    }


def reference(x, p, ln_mix_g, ln_mix_b, ln_ffn_g, ln_ffn_b,
              fox_w_in, fox_b_f, fox_w_o,
              ssd_w_in, ssd_conv_w, ssd_conv_b, ssd_dt_bias, ssd_a_log, ssd_d, ssd_norm_g, ssd_w_out,
              ffn_w_gate, ffn_w_up, ffn_w_down,
              moe_router, moe_w_gate, moe_w_up, moe_w_down,
              ple_w_proj, ple_w_gate):
    h = x
    for i in range(DEPTH):
        j = i // 2
        if i % 2 == 0:
            mix = forgetting_attention(h, fox_w_in[j], fox_b_f[j], fox_w_o[j])
        else:
            mix = ssd_mixer(h, ssd_w_in[j], ssd_conv_w[j], ssd_conv_b[j], ssd_dt_bias[j],
                            ssd_a_log[j], ssd_d[j], ssd_norm_g[j], ssd_w_out[j])
        h = layer_norm(DEEPNORM_ALPHA * h + mix, ln_mix_g[i], ln_mix_b[i])
        if i % 2 == 0:
            ffn = swiglu(h, ffn_w_gate[j], ffn_w_up[j], ffn_w_down[j])
        else:
            ffn = moe_swiglu(h, moe_router[j], moe_w_gate[j], moe_w_up[j], moe_w_down[j])
        h = layer_norm(DEEPNORM_ALPHA * h + ffn, ln_ffn_g[i], ln_ffn_b[i])
        h = h + (p[i] @ ple_w_proj[i]) * jax.nn.sigmoid(h @ ple_w_gate[i])
    return h
```

```python
import functools

import jax
import jax.numpy as jnp
from jax import lax
from jax.experimental import pallas as pl
from jax.experimental.pallas import tpu as pltpu

F32 = jnp.float32
BF16 = jnp.bfloat16
I32 = jnp.int32

LN_EPS = 1e-5
RMS_EPS = 1e-5
LANES = 128
HEAD_DIM = 64
SSD_GROUPS = 8
SSD_STATE = 128
SSD_CONV = 4
SSD_CHUNK = 128
TOP_K = 2
VMEM_LIMIT = 56 * 1024 * 1024


def _cp(sem, vmem=VMEM_LIMIT):
    return pltpu.CompilerParams(dimension_semantics=sem, vmem_limit_bytes=vmem)


def _dot(a, b):
    return jnp.dot(a, b, preferred_element_type=F32)


def _dot_nt(a, b):
    return lax.dot_general(a, b, (((1,), (1,)), ((), ())), preferred_element_type=F32)


def _dot_tn(a, b):
    return lax.dot_general(a, b, (((0,), (0,)), ((), ())), preferred_element_type=F32)


def _split2(v):
    hi = v.astype(BF16)
    lo = (v - hi.astype(F32)).astype(BF16)
    return hi, lo


def _split3(v):
    hi = v.astype(BF16)
    r = v - hi.astype(F32)
    mid = r.astype(BF16)
    lo = (r - mid.astype(F32)).astype(BF16)
    return hi, mid, lo


def _dot_x2w2(x, wh, wl):
    xh, xl = _split2(x)
    return _dot(xh, wh) + _dot(xl, wh) + _dot(xh, wl)


def _layer_norm(v, g, b):
    mu = jnp.mean(v, axis=-1, keepdims=True)
    vc = v - mu
    var = jnp.mean(vc * vc, axis=-1, keepdims=True)
    return vc * lax.rsqrt(var + LN_EPS) * g + b


def _silu(v):
    return v * jax.nn.sigmoid(v)


def _softplus(v):
    return jnp.maximum(v, 0.0) + jnp.log1p(jnp.exp(-jnp.abs(v)))


def _qkv_kernel(x_ref, w_ref, o_ref, *, scale):
    j = pl.program_id(1)
    acc = _dot(x_ref[...].astype(BF16), w_ref[...])
    acc = acc * jnp.where(j == 0, scale, 1.0).astype(F32)
    for c in range(o_ref.shape[0]):
        o_ref[c] = acc[:, c * LANES:(c + 1) * LANES].astype(BF16)


def _qkv_proj(x, w_qkv, tm):
    t, d = x.shape
    n = w_qkv.shape[1]
    tn = d
    cpb = tn // LANES
    return pl.pallas_call(
        functools.partial(_qkv_kernel, scale=HEAD_DIM ** -0.5),
        grid=(t // tm, n // tn),
        in_specs=[pl.BlockSpec((tm, d), lambda i, j: (i, 0)),
                  pl.BlockSpec((d, tn), lambda i, j: (0, j))],
        out_specs=pl.BlockSpec((cpb, tm, LANES), lambda i, j: (j, i, 0)),
        out_shape=jax.ShapeDtypeStruct((n // LANES, t, LANES), BF16),
        compiler_params=_cp(("parallel", "arbitrary")),
        name="fox_qkv_proj",
    )(x, w_qkv)


def _fgate_kernel(x_ref, wh_ref, wl_ref, b_ref, o_ref, carry_ref):
    @pl.when(pl.program_id(1) == 0)
    def _():
        carry_ref[...] = jnp.zeros_like(carry_ref)

    z = _dot_x2w2(x_ref[...], wh_ref[...], wl_ref[...]) + b_ref[...]
    log_f = jnp.minimum(z, 0.0) - jnp.log1p(jnp.exp(-jnp.abs(z)))
    ts = log_f.shape[0]
    r = lax.broadcasted_iota(I32, (ts, ts), 0)
    c = lax.broadcasted_iota(I32, (ts, ts), 1)
    tri = jnp.where(r >= c, 1.0, 0.0).astype(BF16)
    hi, mid, lo = _split3(log_f)
    cs = _dot(tri, hi) + _dot(tri, mid) + _dot(tri, lo) + carry_ref[...]
    o_ref[...] = cs
    carry_ref[...] = cs[ts - 1:ts, :]


def _fgate_cumlog(x, w_f, b_f, bsz, seq, ts):
    t, d = x.shape
    nh = w_f.shape[1]
    w_pad = jnp.zeros((d, LANES), F32).at[:, :nh].set(w_f)
    wh, wl = _split2(w_pad)
    b_pad = jnp.zeros((1, LANES), F32).at[0, :nh].set(b_f)
    nsb = seq // ts
    return pl.pallas_call(
        _fgate_kernel,
        grid=(bsz, nsb),
        in_specs=[pl.BlockSpec((ts, d), lambda b, s: (b * nsb + s, 0)),
                  pl.BlockSpec((d, LANES), lambda b, s: (0, 0)),
                  pl.BlockSpec((d, LANES), lambda b, s: (0, 0)),
                  pl.BlockSpec((1, LANES), lambda b, s: (0, 0))],
        out_specs=pl.BlockSpec((ts, LANES), lambda b, s: (b * nsb + s, 0)),
        out_shape=jax.ShapeDtypeStruct((t, LANES), F32),
        scratch_shapes=[pltpu.VMEM((1, LANES), F32)],
        compiler_params=_cp(("parallel", "arbitrary")),
        name="fox_forget_cumlog",
    )(x, wh, wl, b_pad)


def _fox_kernel(q_ref, k_ref, v_ref, cq_ref, ck_ref, o_ref, *, tq):
    qi = pl.program_id(2)
    q = q_ref[0]
    lane = lax.broadcasted_iota(I32, (tq, LANES), 1)
    first = lane < HEAD_DIM
    zero = jnp.zeros_like(q)
    q_heads = (jnp.where(first, q, zero), jnp.where(first, zero, q))
    cq = cq_ref[0, 0]
    cq_heads = (cq[:, 0:1], cq[:, 1:2])

    def tile(j, carry, diagonal):
        off = pl.multiple_of(j * tq, tq)
        kt = k_ref[0, pl.ds(off, tq), :]
        vt = v_ref[0, pl.ds(off, tq), :]
        ck = ck_ref[0, 0, :, pl.ds(off, tq)]
        new = []
        for h in range(2):
            m, l, acc = carry[h]
            s = _dot_nt(q_heads[h], kt)
            s = s + (cq_heads[h] - ck[h:h + 1, :])
            if diagonal:
                r = lax.broadcasted_iota(I32, (tq, tq), 0)
                c = lax.broadcasted_iota(I32, (tq, tq), 1)
                s = jnp.where(c <= r, s, -jnp.inf)
            m_new = jnp.maximum(m, jnp.max(s, axis=-1, keepdims=True))
            alpha = jnp.exp(m - m_new)
            p = jnp.exp(s - m_new)
            l = alpha * l + jnp.sum(p, axis=-1, keepdims=True)
            acc = alpha * acc + _dot(p.astype(BF16), vt)
            new.append((m_new, l, acc))
        return tuple(new)

    init = tuple((jnp.full((tq, 1), -jnp.inf, F32), jnp.zeros((tq, 1), F32),
                  jnp.zeros((tq, LANES), F32)) for _ in range(2))
    carry = lax.fori_loop(0, qi, lambda j, c: tile(j, c, False), init)
    (_, l0, a0), (_, l1, a1) = tile(qi, carry, True)
    o_ref[...] = jnp.where(first, a0 / l0, a1 / l1).astype(o_ref.dtype)


def _fox_attention(qkv, cum, bsz, seq, tq):
    hp = qkv.shape[0] // 3
    t = qkv.shape[1]
    nh = 2 * hp
    c4 = cum[:, :nh].reshape(bsz, seq, hp, 2)
    cq = c4.transpose(0, 2, 1, 3)
    ck = c4.transpose(0, 2, 3, 1)
    nq = seq // tq
    return pl.pallas_call(
        functools.partial(_fox_kernel, tq=tq),
        grid=(bsz, hp, nq),
        in_specs=[pl.BlockSpec((1, tq, LANES), lambda b, h, i: (h, b * nq + i, 0)),
                  pl.BlockSpec((1, seq, LANES), lambda b, h, i: (hp + h, b, 0)),
                  pl.BlockSpec((1, seq, LANES), lambda b, h, i: (2 * hp + h, b, 0)),
                  pl.BlockSpec((1, 1, tq, 2), lambda b, h, i: (b, h, i, 0)),
                  pl.BlockSpec((1, 1, 2, seq), lambda b, h, i: (b, h, 0, 0))],
        out_specs=pl.BlockSpec((tq, LANES), lambda b, h, i: (b * nq + i, h)),
        out_shape=jax.ShapeDtypeStruct((t, hp * LANES), BF16),
        compiler_params=_cp(("parallel", "parallel", "arbitrary")),
        name="fox_flash_attention",
    )(qkv, qkv, qkv, cq, ck)


def _proj_ln_kernel(a_ref, w_ref, res_ref, g_ref, b_ref, o_ref, *, alpha):
    y = _dot(a_ref[...], w_ref[...])
    o_ref[...] = _layer_norm(alpha * res_ref[...] + y, g_ref[...], b_ref[...])


def _proj_residual_ln(a, w, res, g, b, alpha, tm, name):
    t, k = a.shape
    d = w.shape[1]
    return pl.pallas_call(
        functools.partial(_proj_ln_kernel, alpha=alpha),
        grid=(t // tm,),
        in_specs=[pl.BlockSpec((tm, k), lambda i: (i, 0)),
                  pl.BlockSpec((k, d), lambda i: (0, 0)),
                  pl.BlockSpec((tm, d), lambda i: (i, 0)),
                  pl.BlockSpec((1, d), lambda i: (0, 0)),
                  pl.BlockSpec((1, d), lambda i: (0, 0))],
        out_specs=pl.BlockSpec((tm, d), lambda i: (i, 0)),
        out_shape=jax.ShapeDtypeStruct((t, d), F32),
        compiler_params=_cp(("parallel",)),
        name=name,
    )(a, w, res, g.reshape(1, d), b.reshape(1, d))


def _swiglu_kernel(te_ref, tv_ref, x_ref, wg_ref, wu_ref, wd_ref, o_ref, *, fchunk):
    i = pl.program_id(0)

    @pl.when(tv_ref[i] == 0)
    def _():
        o_ref[...] = jnp.zeros_like(o_ref)

    @pl.when(tv_ref[i] != 0)
    def _():
        x = x_ref[...].astype(BF16)
        f = wg_ref.shape[-1]
        for n, c0 in enumerate(range(0, f, fchunk)):
            c1 = min(c0 + fchunk, f)
            g = _dot(x, wg_ref[0, :, c0:c1])
            u = _dot(x, wu_ref[0, :, c0:c1])
            y = _dot((_silu(g) * u).astype(BF16), wd_ref[0, c0:c1, :])
            if n == 0:
                o_ref[...] = y
            else:
                o_ref[...] += y


def _swiglu_tiles(x, w_gate, w_up, w_down, tile_expert, tile_valid, tm, fchunk, name):
    r, d = x.shape
    f = w_gate.shape[-1]
    once = pl.Buffered(1)
    grid_spec = pltpu.PrefetchScalarGridSpec(
        num_scalar_prefetch=2,
        grid=(r // tm,),
        in_specs=[pl.BlockSpec((tm, d), lambda i, te, tv: (i, 0)),
                  pl.BlockSpec((1, d, f), lambda i, te, tv: (te[i], 0, 0), pipeline_mode=once),
                  pl.BlockSpec((1, d, f), lambda i, te, tv: (te[i], 0, 0), pipeline_mode=once),
                  pl.BlockSpec((1, f, d), lambda i, te, tv: (te[i], 0, 0), pipeline_mode=once)],
        out_specs=pl.BlockSpec((tm, d), lambda i, te, tv: (i, 0)),
    )
    return pl.pallas_call(
        functools.partial(_swiglu_kernel, fchunk=fchunk),
        grid_spec=grid_spec,
        out_shape=jax.ShapeDtypeStruct((r, d), F32),
        compiler_params=_cp(("arbitrary",)),
        name=name,
    )(tile_expert, tile_valid, x, w_gate, w_up, w_down)


def _ple(h, p_ref, wp_ref, wgt_ref):
    gate = jax.nn.sigmoid(_dot(h.astype(BF16), wgt_ref[...]))
    proj = _dot(p_ref[...].astype(BF16), wp_ref[...])
    return h + proj * gate


def _post_dense_kernel(res_ref, y_ref, p_ref, g_ref, b_ref, wp_ref, wgt_ref, o_ref, *, alpha):
    h = _layer_norm(alpha * res_ref[...] + y_ref[...], g_ref[...], b_ref[...])
    o_ref[...] = _ple(h, p_ref, wp_ref, wgt_ref)


def _post_dense(res, y, p, g, b, w_proj, w_gate, alpha, tm):
    t, d = res.shape
    pd = p.shape[1]
    row = lambda i: (i, 0)
    fixed = lambda i: (0, 0)
    return pl.pallas_call(
        functools.partial(_post_dense_kernel, alpha=alpha),
        grid=(t // tm,),
        in_specs=[pl.BlockSpec((tm, d), row), pl.BlockSpec((tm, d), row),
                  pl.BlockSpec((tm, pd), row),
                  pl.BlockSpec((1, d), fixed), pl.BlockSpec((1, d), fixed),
                  pl.BlockSpec((pd, d), fixed), pl.BlockSpec((d, d), fixed)],
        out_specs=pl.BlockSpec((tm, d), row),
        out_shape=jax.ShapeDtypeStruct((t, d), F32),
        compiler_params=_cp(("parallel",)),
        name="post_ffn_dense",
    )(res, y, p, g.reshape(1, d), b.reshape(1, d), w_proj, w_gate)


def _row_copy(src_hbm, row, dst, r, sem):
    return pltpu.make_async_copy(src_hbm.at[pl.ds(row, 1), :], dst.at[pl.ds(r, 1), :], sem)


def _post_moe_kernel(slot_ref, res_ref, y_hbm, gw_ref, p_ref, g_ref, b_ref, wp_ref, wgt_ref,
                     o_ref, buf_ref, sem, *, alpha):
    tm = res_ref.shape[0]

    def start(r, _):
        for k in range(TOP_K):
            _row_copy(y_hbm, slot_ref[0, k, r], buf_ref.at[k], r, sem.at[k]).start()
        return 0

    def wait(r, _):
        for k in range(TOP_K):
            _row_copy(y_hbm, slot_ref[0, k, r], buf_ref.at[k], r, sem.at[k]).wait()
        return 0

    lax.fori_loop(0, tm, start, 0)
    lax.fori_loop(0, tm, wait, 0)
    gw = gw_ref[...]
    ffn = gw[:, 0:1] * buf_ref[0] + gw[:, 1:2] * buf_ref[1]
    h = _layer_norm(alpha * res_ref[...] + ffn, g_ref[...], b_ref[...])
    o_ref[...] = _ple(h, p_ref, wp_ref, wgt_ref)


def _post_moe(res, y_slots, slots, gate_w, p, g, b, w_proj, w_gate, alpha, tm):
    t, d = res.shape
    pd = p.shape[1]
    row = lambda i: (i, 0)
    fixed = lambda i: (0, 0)
    slots3 = slots.reshape(TOP_K, t // tm, tm).transpose(1, 0, 2)
    return pl.pallas_call(
        functools.partial(_post_moe_kernel, alpha=alpha),
        grid=(t // tm,),
        in_specs=[pl.BlockSpec((1, TOP_K, tm), lambda i: (i, 0, 0), memory_space=pltpu.SMEM),
                  pl.BlockSpec((tm, d), row),
                  pl.BlockSpec(memory_space=pl.ANY),
                  pl.BlockSpec((tm, TOP_K), row),
                  pl.BlockSpec((tm, pd), row),
                  pl.BlockSpec((1, d), fixed), pl.BlockSpec((1, d), fixed),
                  pl.BlockSpec((pd, d), fixed), pl.BlockSpec((d, d), fixed)],
        out_specs=pl.BlockSpec((tm, d), row),
        out_shape=jax.ShapeDtypeStruct((t, d), F32),
        scratch_shapes=[pltpu.VMEM((TOP_K, tm, d), F32), pltpu.SemaphoreType.DMA((TOP_K,))],
        compiler_params=_cp(("arbitrary",)),
        name="post_ffn_moe_combine",
    )(slots3, res, y_slots, gate_w, p, g.reshape(1, d), b.reshape(1, d), w_proj, w_gate)


def _inproj_kernel(x_ref, w_ref, o_ref):
    o_ref[...] = _dot(x_ref[...].astype(BF16), w_ref[...]).astype(o_ref.dtype)


def _inproj_conv_kernel(x_ref, w_ref, cw_ref, cb_ref, o_ref, ext_ref, carry_ref, *, tiles_per_seq):
    i = pl.program_id(0)
    j = pl.program_id(1)
    tm = x_ref.shape[0]
    halo = carry_ref.shape[1]

    @pl.when(i % tiles_per_seq == 0)
    def _():
        carry_ref[j] = jnp.zeros(carry_ref.shape[1:], F32)

    acc = _dot(x_ref[...].astype(BF16), w_ref[...])
    ext_ref[0:halo, :] = carry_ref[j]
    ext_ref[halo:halo + tm, :] = acc
    cw = cw_ref[...]
    out = cb_ref[...] + cw[0:1, :] * ext_ref[halo - 3:halo - 3 + tm, :]
    for k in range(1, SSD_CONV):
        out = out + cw[k:k + 1, :] * ext_ref[halo - 3 + k:halo - 3 + k + tm, :]
    carry_ref[j] = acc[tm - halo:tm, :]
    o_ref[...] = _silu(out).astype(o_ref.dtype)


def _ssd_inproj(x, w, tm, tn, name):
    t, d = x.shape
    n = w.shape[1]
    return pl.pallas_call(
        _inproj_kernel,
        grid=(t // tm, n // tn),
        in_specs=[pl.BlockSpec((tm, d), lambda i, j: (i, 0)),
                  pl.BlockSpec((d, tn), lambda i, j: (0, j))],
        out_specs=pl.BlockSpec((tm, tn), lambda i, j: (i, j)),
        out_shape=jax.ShapeDtypeStruct((t, n), BF16),
        compiler_params=_cp(("parallel", "arbitrary")),
        name=name,
    )(x, w)


def _ssd_inproj_conv(x, w, conv_w, conv_b, seq, tm, tn):
    t, d = x.shape
    n = w.shape[1]
    halo = 8
    return pl.pallas_call(
        functools.partial(_inproj_conv_kernel, tiles_per_seq=seq // tm),
        grid=(t // tm, n // tn),
        in_specs=[pl.BlockSpec((tm, d), lambda i, j: (i, 0)),
                  pl.BlockSpec((d, tn), lambda i, j: (0, j)),
                  pl.BlockSpec((SSD_CONV, tn), lambda i, j: (0, j)),
                  pl.BlockSpec((1, tn), lambda i, j: (0, j))],
        out_specs=pl.BlockSpec((tm, tn), lambda i, j: (i, j)),
        out_shape=jax.ShapeDtypeStruct((t, n), BF16),
        scratch_shapes=[pltpu.VMEM((tm + halo, tn), F32), pltpu.VMEM((n // tn, halo, tn), F32)],
        compiler_params=_cp(("arbitrary", "arbitrary")),
        name="ssd_inproj_conv_silu",
    )(x, w, conv_w, conv_b.reshape(1, n))


def _ssd_dt_kernel(x_ref, wh_ref, wl_ref, bias_ref, alog_ref, dt_ref, cs_ref):
    dt = _softplus(_dot_x2w2(x_ref[...], wh_ref[...], wl_ref[...]) + bias_ref[...])
    da = dt * (-jnp.exp(alog_ref[...]))
    ts = da.shape[0]
    r = lax.broadcasted_iota(I32, (ts, ts), 0)
    c = lax.broadcasted_iota(I32, (ts, ts), 1)
    same_chunk = (r // SSD_CHUNK) == (c // SSD_CHUNK)
    tri = jnp.where(jnp.logical_and(r >= c, same_chunk), 1.0, 0.0).astype(BF16)
    hi, mid, lo = _split3(da)
    dt_ref[...] = dt
    cs_ref[...] = _dot(tri, hi) + _dot(tri, mid) + _dot(tri, lo)


def _ssd_dt(x, w_dt, dt_bias, a_log, ts):
    t, d = x.shape
    nh = w_dt.shape[1]
    w_pad = jnp.zeros((d, LANES), F32).at[:, :nh].set(w_dt)
    wh, wl = _split2(w_pad)
    bias = jnp.zeros((1, LANES), F32).at[0, :nh].set(dt_bias)
    alog = jnp.zeros((1, LANES), F32).at[0, :nh].set(a_log)
    fixed = lambda i: (0, 0)
    row = lambda i: (i, 0)
    return pl.pallas_call(
        _ssd_dt_kernel,
        grid=(t // ts,),
        in_specs=[pl.BlockSpec((ts, d), row), pl.BlockSpec((d, LANES), fixed),
                  pl.BlockSpec((d, LANES), fixed), pl.BlockSpec((1, LANES), fixed),
                  pl.BlockSpec((1, LANES), fixed)],
        out_specs=[pl.BlockSpec((ts, LANES), row), pl.BlockSpec((ts, LANES), row)],
        out_shape=[jax.ShapeDtypeStruct((t, LANES), F32), jax.ShapeDtypeStruct((t, LANES), F32)],
        compiler_params=_cp(("parallel",)),
        name="ssd_dt_prep",
    )(x, wh, wl, bias, alog)


def _ssd_scan_kernel(xs_ref, bm_ref, cm_ref, z_ref, dt_ref, cs_ref, cst_ref, d_ref, ng_ref,
                     o_ref, h_ref):
    @pl.when(pl.program_id(2) == 0)
    def _():
        h_ref[...] = jnp.zeros_like(h_ref)

    lb, gw = xs_ref.shape
    hpg = gw // HEAD_DIM
    cl = SSD_CHUNK
    lane_head = lax.broadcasted_iota(I32, (cl, gw), 1) // HEAD_DIM
    row_head = lax.broadcasted_iota(I32, (1, gw), 1) // HEAD_DIM
    r = lax.broadcasted_iota(I32, (cl, cl), 0)
    c = lax.broadcasted_iota(I32, (cl, cl), 1)
    causal = r >= c

    def expand(cols, heads):
        out = cols[:, hpg - 1:hpg]
        for hh in range(hpg - 2, -1, -1):
            out = jnp.where(heads == hh, cols[:, hh:hh + 1], out)
        return out

    for ci in range(lb // cl):
        sl = slice(ci * cl, (ci + 1) * cl)
        x = xs_ref[sl, :].astype(F32)
        bm = bm_ref[sl, :]
        cm = cm_ref[sl, :]
        dt = dt_ref[0, sl, :]
        cs = cs_ref[0, sl, :]
        cst = cst_ref[0, :, sl]
        cs_e = expand(cs, lane_head)
        end_e = expand(cs[cl - 1:cl, :], row_head)
        xdt = x * expand(dt, lane_head)
        cb = _dot_nt(cm, bm)
        ms = []
        for hh in range(hpg):
            seg = cs[:, hh:hh + 1] - cst[hh:hh + 1, :]
            decay = jnp.exp(jnp.where(causal, seg, -jnp.inf))
            ms.append((cb * decay).astype(BF16))
        y_all = _dot(jnp.concatenate(ms, axis=0), xdt.astype(BF16))
        y = y_all[(hpg - 1) * cl:hpg * cl, :]
        for hh in range(hpg - 2, -1, -1):
            y = jnp.where(lane_head == hh, y_all[hh * cl:(hh + 1) * cl, :], y)
        h_in = h_ref[...]
        y = y + _dot(cm, h_in.astype(BF16)) * jnp.exp(cs_e)
        y = y + d_ref[0] * x
        states = _dot_tn(bm, (xdt * jnp.exp(end_e - cs_e)).astype(BF16))
        h_ref[...] = jnp.exp(end_e) * h_in + states
        y = y * _silu(z_ref[sl, :].astype(F32))
        y = y * lax.rsqrt(jnp.mean(y * y, axis=-1, keepdims=True) + RMS_EPS)
        o_ref[sl, :] = (y * ng_ref[...]).astype(o_ref.dtype)


def _ssd_scan(xbc, z, dt, cs, d_skip, norm_g, bsz, seq, lb):
    t = xbc.shape[0]
    d_inner = z.shape[1]
    g = SSD_GROUPS
    gw = d_inner // g
    hpg = gw // HEAD_DIM
    n = SSD_STATE
    nsb = seq // lb
    dt_g = dt.reshape(t, g, hpg).transpose(1, 0, 2)
    cs_g = cs.reshape(t, g, hpg).transpose(1, 0, 2)
    cs_t = cs.reshape(t, g, hpg).transpose(1, 2, 0)
    d_e = jnp.repeat(d_skip.astype(F32), HEAD_DIM).reshape(g, 1, gw)
    rows = lambda b, gi, s: b * nsb + s
    return pl.pallas_call(
        _ssd_scan_kernel,
        grid=(bsz, g, nsb),
        in_specs=[pl.BlockSpec((lb, gw), lambda b, gi, s: (rows(b, gi, s), gi)),
                  pl.BlockSpec((lb, n), lambda b, gi, s: (rows(b, gi, s), d_inner // n + gi)),
                  pl.BlockSpec((lb, n), lambda b, gi, s: (rows(b, gi, s), d_inner // n + g + gi)),
                  pl.BlockSpec((lb, gw), lambda b, gi, s: (rows(b, gi, s), gi)),
                  pl.BlockSpec((1, lb, hpg), lambda b, gi, s: (gi, rows(b, gi, s), 0)),
                  pl.BlockSpec((1, lb, hpg), lambda b, gi, s: (gi, rows(b, gi, s), 0)),
                  pl.BlockSpec((1, hpg, lb), lambda b, gi, s: (gi, 0, rows(b, gi, s))),
                  pl.BlockSpec((1, 1, gw), lambda b, gi, s: (gi, 0, 0)),
                  pl.BlockSpec((1, gw), lambda b, gi, s: (0, gi))],
        out_specs=pl.BlockSpec((lb, gw), lambda b, gi, s: (rows(b, gi, s), gi)),
        out_shape=jax.ShapeDtypeStruct((t, d_inner), BF16),
        scratch_shapes=[pltpu.VMEM((n, gw), F32)],
        compiler_params=_cp(("parallel", "parallel", "arbitrary")),
        name="ssd_chunk_scan",
    )(xbc, xbc, xbc, z, dt_g, cs_g, cs_t, d_e, norm_g.reshape(1, d_inner))


def _route_kernel(x_ref, rh_ref, rl_ref, idx_ref, gw_ref, rank_ref, cnt_ref, carry_ref):
    @pl.when(pl.program_id(0) == 0)
    def _():
        carry_ref[...] = jnp.zeros_like(carry_ref)

    xh, xl = _split2(x_ref[...])
    rh = rh_ref[...]
    logits = _dot_nt(rh, xh) + _dot_nt(rh, xl) + _dot_nt(rl_ref[...], xh)
    ne, tm = logits.shape
    e = lax.broadcasted_iota(I32, (ne, tm), 0)
    m1 = jnp.max(logits, axis=0, keepdims=True)
    i1 = jnp.min(jnp.where(logits == m1, e, ne), axis=0, keepdims=True)
    rest = jnp.where(e == i1, -jnp.inf, logits)
    m2 = jnp.max(rest, axis=0, keepdims=True)
    i2 = jnp.min(jnp.where(rest == m2, e, ne), axis=0, keepdims=True)
    ex = jnp.exp(m2 - m1)
    w1 = 1.0 / (1.0 + ex)
    w2 = ex / (1.0 + ex)
    sel1 = e == i1
    sel2 = e == i2
    sel = jnp.where(jnp.logical_or(sel1, sel2), 1.0, 0.0)
    r = lax.broadcasted_iota(I32, (tm, tm), 0)
    c = lax.broadcasted_iota(I32, (tm, tm), 1)
    before = jnp.where(r < c, 1.0, 0.0).astype(BF16)
    base = carry_ref[...][:, 0:1]
    rank = _dot(sel.astype(BF16), before) + base
    r1 = jnp.sum(jnp.where(sel1, rank, 0.0), axis=0, keepdims=True)
    r2 = jnp.sum(jnp.where(sel2, rank, 0.0), axis=0, keepdims=True)
    idx_ref[...] = jnp.concatenate([i1, i2], axis=0)
    gw_ref[...] = jnp.concatenate([w1, w2], axis=0)
    rank_ref[...] = jnp.concatenate([r1, r2], axis=0).astype(I32)
    total = carry_ref[...] + jnp.sum(sel, axis=1, keepdims=True)
    carry_ref[...] = total
    cnt_ref[...] = total.astype(I32)


def _moe_route(x, router, tm):
    t, d = x.shape
    ne = router.shape[1]
    rh, rl = _split2(router.T)
    fixed = lambda i: (0, 0)
    col = lambda i: (0, i)
    return pl.pallas_call(
        _route_kernel,
        grid=(t // tm,),
        in_specs=[pl.BlockSpec((tm, d), lambda i: (i, 0)),
                  pl.BlockSpec((ne, d), fixed), pl.BlockSpec((ne, d), fixed)],
        out_specs=[pl.BlockSpec((TOP_K, tm), col), pl.BlockSpec((TOP_K, tm), col),
                   pl.BlockSpec((TOP_K, tm), col), pl.BlockSpec((ne, LANES), fixed)],
        out_shape=[jax.ShapeDtypeStruct((TOP_K, t), I32), jax.ShapeDtypeStruct((TOP_K, t), F32),
                   jax.ShapeDtypeStruct((TOP_K, t), I32), jax.ShapeDtypeStruct((ne, LANES), I32)],
        scratch_shapes=[pltpu.VMEM((ne, LANES), F32)],
        compiler_params=_cp(("arbitrary",)),
        name="moe_route_top2",
    )(x, rh, rl)


def _gather_rows_kernel(tok_ref, x_hbm, o_ref, buf_ref, sem):
    tm = o_ref.shape[0]

    def start(r, _):
        _row_copy(x_hbm, tok_ref[0, 0, r], buf_ref, r, sem).start()
        return 0

    def wait(r, _):
        _row_copy(x_hbm, tok_ref[0, 0, r], buf_ref, r, sem).wait()
        return 0

    lax.fori_loop(0, tm, start, 0)
    lax.fori_loop(0, tm, wait, 0)
    o_ref[...] = buf_ref[...].astype(o_ref.dtype)


def _gather_rows(x, token_of_slot, tm):
    d = x.shape[1]
    ns = token_of_slot.shape[0]
    return pl.pallas_call(
        _gather_rows_kernel,
        grid=(ns // tm,),
        in_specs=[pl.BlockSpec((1, 1, tm), lambda i: (i, 0, 0), memory_space=pltpu.SMEM),
                  pl.BlockSpec(memory_space=pl.ANY)],
        out_specs=pl.BlockSpec((tm, d), lambda i: (i, 0)),
        out_shape=jax.ShapeDtypeStruct((ns, d), BF16),
        scratch_shapes=[pltpu.VMEM((tm, d), F32), pltpu.SemaphoreType.DMA(())],
        compiler_params=_cp(("arbitrary",)),
        name="moe_dispatch_gather",
    )(token_of_slot.reshape(ns // tm, 1, tm), x)


def _group_plan(idx, rank, counts, tm, n_tiles):
    ne = counts.shape[0]
    t = idx.shape[1]
    tiles_e = (counts + tm - 1) // tm
    tile_end = jnp.cumsum(tiles_e)
    tile_start = tile_end - tiles_e
    slots = tile_start[idx] * tm + rank
    tile_ids = jnp.arange(n_tiles, dtype=I32)
    tile_expert = jnp.minimum(jnp.searchsorted(tile_end, tile_ids, side="right"), ne - 1).astype(I32)
    tile_valid = (tile_ids < tile_end[ne - 1]).astype(I32)
    tokens = jnp.broadcast_to(jnp.arange(t, dtype=I32), (TOP_K, t))
    token_of_slot = jnp.zeros((n_tiles * tm,), I32).at[slots.reshape(-1)].set(tokens.reshape(-1))
    return slots.astype(I32), tile_expert, tile_valid, token_of_slot


def _pick(pref, n):
    return pref if n % pref == 0 else n


def kernel(x, p, ln_mix_g, ln_mix_b, ln_ffn_g, ln_ffn_b, fox_w_in, fox_b_f, fox_w_o, ssd_w_in, ssd_conv_w, ssd_conv_b, ssd_dt_bias, ssd_a_log, ssd_d, ssd_norm_g, ssd_w_out, ffn_w_gate, ffn_w_up, ffn_w_down, moe_router, moe_w_gate, moe_w_up, moe_w_down, ple_w_proj, ple_w_gate):
    bsz, seq, d = x.shape
    depth = p.shape[0]
    t = bsz * seq
    alpha = (2.0 * depth) ** 0.25
    x2 = x.reshape(t, d)
    p2 = p.reshape(depth, t, p.shape[-1])
    tm_big = _pick(1024, t)
    tm = _pick(512, t)
    ts = _pick(512, seq)
    one_tile = jnp.zeros((t // tm,), I32)
    all_valid = jnp.ones((t // tm,), I32)

    w_in = fox_w_in[0]
    qkv = _qkv_proj(x2, w_in[:, :3 * d].astype(BF16), tm_big)
    cum = _fgate_cumlog(x2, w_in[:, 3 * d:], fox_b_f[0], bsz, seq, ts)
    attn = _fox_attention(qkv, cum, bsz, seq, ts)
    h = _proj_residual_ln(attn, fox_w_o[0].astype(BF16), x2, ln_mix_g[0], ln_mix_b[0], alpha, tm,
                          "fox_out_proj_ln")
    y = _swiglu_tiles(h, ffn_w_gate.astype(BF16), ffn_w_up.astype(BF16), ffn_w_down.astype(BF16),
                      one_tile, all_valid, tm, 256, "ffn_swiglu_dense")
    h = _post_dense(h, y, p2[0], ln_ffn_g[0], ln_ffn_b[0], ple_w_proj[0].astype(BF16),
                    ple_w_gate[0].astype(BF16), alpha, tm)

    w_in = ssd_w_in[0]
    d_inner = ssd_norm_g.shape[1]
    conv_ch = ssd_conv_w.shape[2]
    z = _ssd_inproj(h, w_in[:, :d_inner].astype(BF16), tm_big, _pick(1024, d_inner), "ssd_inproj_z")
    xbc = _ssd_inproj_conv(h, w_in[:, d_inner:d_inner + conv_ch].astype(BF16), ssd_conv_w[0],
                           ssd_conv_b[0], seq, _pick(1024, seq), _pick(1024, conv_ch))
    dt, cs = _ssd_dt(h, w_in[:, d_inner + conv_ch:], ssd_dt_bias[0], ssd_a_log[0], ts)
    nh = ssd_dt_bias.shape[1]
    ymix = _ssd_scan(xbc, z, dt[:, :nh], cs[:, :nh], ssd_d[0], ssd_norm_g[0], bsz, seq, ts)
    h = _proj_residual_ln(ymix, ssd_w_out[0].astype(BF16), h, ln_mix_g[1], ln_mix_b[1], alpha, tm,
                          "ssd_out_proj_ln")
    ne = moe_router.shape[2]
    idx, gate_w, rank, counts = _moe_route(h, moe_router[0], tm)
    n_tiles = (TOP_K * t) // tm + ne
    slots, tile_expert, tile_valid, token_of_slot = _group_plan(idx, rank, counts[:, 0], tm, n_tiles)
    xs = _gather_rows(h, token_of_slot, tm)
    ys = _swiglu_tiles(xs, moe_w_gate[0].astype(BF16), moe_w_up[0].astype(BF16),
                       moe_w_down[0].astype(BF16), tile_expert, tile_valid, tm, 512,
                       "moe_swiglu_experts")
    tmc = _pick(256, t)
    h = _post_moe(h, ys, slots, gate_w.T, p2[1], ln_ffn_g[1], ln_ffn_b[1],
                  ple_w_proj[1].astype(BF16), ple_w_gate[1].astype(BF16), alpha, tmc)
    return h.reshape(bsz, seq, d)
```

```python
import functools

import jax
import jax.numpy as jnp
from jax import lax
from jax.experimental import pallas as pl
from jax.experimental.pallas import tpu as pltpu

F32 = jnp.float32
BF16 = jnp.bfloat16
I32 = jnp.int32

LN_EPS = 1e-5
RMS_EPS = 1e-5
LANES = 128
HEAD_DIM = 64
SSD_GROUPS = 8
SSD_STATE = 128
SSD_CONV = 4
SSD_CHUNK = 128
TOP_K = 2
LOG2E = 1.4426950408889634
VMEM_LIMIT = 56 * 1024 * 1024


def _cp(sem, vmem=VMEM_LIMIT):
    return pltpu.CompilerParams(dimension_semantics=sem, vmem_limit_bytes=vmem)


def _dot(a, b):
    return jnp.dot(a, b, preferred_element_type=F32)


def _dot_nt(a, b):
    return lax.dot_general(a, b, (((1,), (1,)), ((), ())), preferred_element_type=F32)


def _dot_tn(a, b):
    return lax.dot_general(a, b, (((0,), (0,)), ((), ())), preferred_element_type=F32)


def _split2(v):
    hi = v.astype(BF16)
    lo = (v - hi.astype(F32)).astype(BF16)
    return hi, lo


def _split3(v):
    hi = v.astype(BF16)
    r = v - hi.astype(F32)
    mid = r.astype(BF16)
    lo = (r - mid.astype(F32)).astype(BF16)
    return hi, mid, lo


def _dot_x2w2(x, wh, wl):
    xh, xl = _split2(x)
    return _dot(xh, wh) + _dot(xl, wh) + _dot(xh, wl)


def _layer_norm(v, g, b):
    mu = jnp.mean(v, axis=-1, keepdims=True)
    vc = v - mu
    var = jnp.mean(vc * vc, axis=-1, keepdims=True)
    return vc * lax.rsqrt(var + LN_EPS) * g + b


def _silu(v):
    return v * jax.nn.sigmoid(v)


def _softplus(v):
    return jnp.maximum(v, 0.0) + jnp.log1p(jnp.exp(-jnp.abs(v)))


def _qkv_kernel(x_ref, w_ref, o_ref, *, scale):
    j = pl.program_id(1)
    acc = _dot(x_ref[...].astype(BF16), w_ref[...])
    acc = acc * jnp.where(j == 0, scale, 1.0).astype(F32)
    for c in range(o_ref.shape[0]):
        o_ref[c] = acc[:, c * LANES:(c + 1) * LANES].astype(BF16)


def _qkv_proj(x, w_qkv, tm):
    t, d = x.shape
    n = w_qkv.shape[1]
    tn = d
    cpb = tn // LANES
    return pl.pallas_call(
        functools.partial(_qkv_kernel, scale=LOG2E * HEAD_DIM ** -0.5),
        grid=(t // tm, n // tn),
        in_specs=[pl.BlockSpec((tm, d), lambda i, j: (i, 0)),
                  pl.BlockSpec((d, tn), lambda i, j: (0, j))],
        out_specs=pl.BlockSpec((cpb, tm, LANES), lambda i, j: (j, i, 0)),
        out_shape=jax.ShapeDtypeStruct((n // LANES, t, LANES), BF16),
        compiler_params=_cp(("parallel", "arbitrary")),
        name="fox_qkv_proj",
    )(x, w_qkv)


def _fgate_kernel(x_ref, wh_ref, wl_ref, b_ref, o_ref, carry_ref):
    @pl.when(pl.program_id(1) == 0)
    def _():
        carry_ref[...] = jnp.zeros_like(carry_ref)

    z = _dot_x2w2(x_ref[...], wh_ref[...], wl_ref[...]) + b_ref[...]
    log_f = jnp.minimum(z, 0.0) - jnp.log1p(jnp.exp(-jnp.abs(z)))
    ts = log_f.shape[0]
    r = lax.broadcasted_iota(I32, (ts, ts), 0)
    c = lax.broadcasted_iota(I32, (ts, ts), 1)
    tri = jnp.where(r >= c, 1.0, 0.0).astype(BF16)
    hi, mid, lo = _split3(log_f)
    cs = _dot(tri, hi) + _dot(tri, mid) + _dot(tri, lo) + carry_ref[...]
    o_ref[...] = cs
    carry_ref[...] = cs[ts - 1:ts, :]


def _fgate_cumlog(x, w_f, b_f, bsz, seq, ts):
    t, d = x.shape
    nh = w_f.shape[1]
    w_pad = jnp.zeros((d, LANES), F32).at[:, :nh].set(w_f)
    wh, wl = _split2(w_pad)
    b_pad = jnp.zeros((1, LANES), F32).at[0, :nh].set(b_f)
    nsb = seq // ts
    return pl.pallas_call(
        _fgate_kernel,
        grid=(bsz, nsb),
        in_specs=[pl.BlockSpec((ts, d), lambda b, s: (b * nsb + s, 0)),
                  pl.BlockSpec((d, LANES), lambda b, s: (0, 0)),
                  pl.BlockSpec((d, LANES), lambda b, s: (0, 0)),
                  pl.BlockSpec((1, LANES), lambda b, s: (0, 0))],
        out_specs=pl.BlockSpec((ts, LANES), lambda b, s: (b * nsb + s, 0)),
        out_shape=jax.ShapeDtypeStruct((t, LANES), F32),
        scratch_shapes=[pltpu.VMEM((1, LANES), F32)],
        compiler_params=_cp(("parallel", "arbitrary")),
        name="fox_forget_cumlog",
    )(x, wh, wl, b_pad)


def _pieces3(v):
    hi = v.astype(BF16).astype(F32)
    r = v - hi
    mid = r.astype(BF16).astype(F32)
    lo = (r - mid).astype(BF16).astype(F32)
    return hi, mid, lo


def _lane_columns(lane, base, cols):
    out = jnp.zeros(lane.shape, F32)
    for i, col in enumerate(cols):
        out = jnp.where(lane == base + i, col, out)
    return out


def _fox_kernel(q_ref, k_ref, v_ref, c_ref, o_ref, qa_ref, ka_ref, vt_ref, s_ref, p_ref,
                m_ref, alpha_ref, acc_ref, *, tq, cb):
    qi = pl.program_id(2)
    seq = k_ref.shape[1]
    bases = (HEAD_DIM, 0)

    def data_lanes(lane, h):
        return (lane < HEAD_DIM) if h == 0 else (lane >= HEAD_DIM)

    @pl.when(qi == 0)
    def _():
        for c0 in range(0, seq, tq):
            rows = slice(c0, c0 + tq)
            k = k_ref[0, rows, :]
            v = v_ref[0, rows, :].astype(F32)
            c2 = c_ref[0, 0, rows, :] * LOG2E
            lane = lax.broadcasted_iota(I32, k.shape, 1)
            for h in range(2):
                hi, mid, lo = _pieces3(-c2[:, h:h + 1])
                k_aug = _lane_columns(lane, bases[h], [1.0, 1.0, 1.0, hi, mid, lo])
                v_aug = _lane_columns(lane, bases[h], [1.0])
                ka_ref[h, rows, :] = jnp.where(data_lanes(lane, h), k, k_aug.astype(BF16))
                vt_ref[h, :, rows] = jnp.where(data_lanes(lane, h), v, v_aug).T.astype(BF16)

    q = q_ref[0]
    lane = lax.broadcasted_iota(I32, q.shape, 1)
    cq2 = c_ref[0, 0, pl.ds(pl.multiple_of(qi * tq, tq), tq), :] * LOG2E
    for h in range(2):
        hi, mid, lo = _pieces3(cq2[:, h:h + 1])
        q_aug = _lane_columns(lane, bases[h], [hi, mid, lo, 1.0, 1.0, 1.0])
        qa_ref[h] = jnp.where(data_lanes(lane, h), q, q_aug.astype(BF16))
    m_ref[...] = jnp.full(m_ref.shape, -jnp.inf, F32)
    acc_ref[...] = jnp.zeros(acc_ref.shape, F32)

    def tile(j, diagonal):
        off = pl.multiple_of(j * tq, tq)
        for h in range(2):
            s_ref[h] = _dot_nt(ka_ref[h, pl.ds(off, tq), :], qa_ref[h])
        for h in range(2):
            for c0 in range(0, tq, cb):
                cols = slice(c0, c0 + cb)
                nrow = c0 + cb if diagonal else tq
                s = s_ref[h, 0:nrow, cols]
                if diagonal:
                    key = lax.broadcasted_iota(I32, s.shape, 0)
                    qry = c0 + lax.broadcasted_iota(I32, s.shape, 1)
                    s = jnp.where(key <= qry, s, -jnp.inf)
                m_old = m_ref[h, :, cols]
                m_new = jnp.maximum(m_old, jnp.max(s, axis=0, keepdims=True))
                m_ref[h, :, cols] = m_new
                alpha_ref[h, :, cols] = jnp.exp2(m_old - m_new)
                p_ref[h, 0:nrow, cols] = jnp.exp2(s - m_new).astype(BF16)
                if nrow < tq:
                    p_ref[h, nrow:tq, cols] = jnp.zeros((tq - nrow, cb), BF16)
            acc_ref[h] = alpha_ref[h] * acc_ref[h] + _dot(vt_ref[h, :, pl.ds(off, tq)], p_ref[h])

    def body(j, carry):
        tile(j, False)
        return carry

    lax.fori_loop(0, qi, body, 0)
    tile(qi, True)
    a0 = acc_ref[0]
    a1 = acc_ref[1]
    row = lax.broadcasted_iota(I32, a0.shape, 0)
    out_t = jnp.where(row < HEAD_DIM, a0 / a0[bases[0]:bases[0] + 1, :],
                      a1 / a1[bases[1]:bases[1] + 1, :])
    o_ref[...] = out_t.T.astype(o_ref.dtype)


def _fox_attention(qkv, cum, bsz, seq, tq):
    hp = qkv.shape[0] // 3
    t = qkv.shape[1]
    c4 = cum[:, :2 * hp].reshape(bsz, seq, hp, 2).transpose(0, 2, 1, 3)
    nq = seq // tq
    return pl.pallas_call(
        functools.partial(_fox_kernel, tq=tq, cb=LANES),
        grid=(bsz, hp, nq),
        in_specs=[pl.BlockSpec((1, tq, LANES), lambda b, h, i: (h, b * nq + i, 0)),
                  pl.BlockSpec((1, seq, LANES), lambda b, h, i: (hp + h, b, 0)),
                  pl.BlockSpec((1, seq, LANES), lambda b, h, i: (2 * hp + h, b, 0)),
                  pl.BlockSpec((1, 1, seq, 2), lambda b, h, i: (b, h, 0, 0))],
        out_specs=pl.BlockSpec((tq, LANES), lambda b, h, i: (b * nq + i, h)),
        out_shape=jax.ShapeDtypeStruct((t, hp * LANES), BF16),
        scratch_shapes=[pltpu.VMEM((2, tq, LANES), BF16),
                        pltpu.VMEM((2, seq, LANES), BF16),
                        pltpu.VMEM((2, LANES, seq), BF16),
                        pltpu.VMEM((2, tq, tq), F32),
                        pltpu.VMEM((2, tq, tq), BF16),
                        pltpu.VMEM((2, 1, tq), F32),
                        pltpu.VMEM((2, 1, tq), F32),
                        pltpu.VMEM((2, LANES, tq), F32)],
        compiler_params=_cp(("parallel", "parallel", "arbitrary")),
        name="fox_flash_attention",
    )(qkv, qkv, qkv, c4)


def _proj_ln_kernel(a_ref, w_ref, res_ref, g_ref, b_ref, o_ref, *, alpha):
    y = _dot(a_ref[...], w_ref[...])
    o_ref[...] = _layer_norm(alpha * res_ref[...] + y, g_ref[...], b_ref[...])


def _proj_residual_ln(a, w, res, g, b, alpha, tm, name):
    t, k = a.shape
    d = w.shape[1]
    return pl.pallas_call(
        functools.partial(_proj_ln_kernel, alpha=alpha),
        grid=(t // tm,),
        in_specs=[pl.BlockSpec((tm, k), lambda i: (i, 0)),
                  pl.BlockSpec((k, d), lambda i: (0, 0)),
                  pl.BlockSpec((tm, d), lambda i: (i, 0)),
                  pl.BlockSpec((1, d), lambda i: (0, 0)),
                  pl.BlockSpec((1, d), lambda i: (0, 0))],
        out_specs=pl.BlockSpec((tm, d), lambda i: (i, 0)),
        out_shape=jax.ShapeDtypeStruct((t, d), F32),
        compiler_params=_cp(("parallel",)),
        name=name,
    )(a, w, res, g.reshape(1, d), b.reshape(1, d))


def _swiglu_kernel(te_ref, tv_ref, x_ref, wg_ref, wu_ref, wd_ref, o_ref, *, fchunk):
    i = pl.program_id(0)

    @pl.when(tv_ref[i] == 0)
    def _():
        o_ref[...] = jnp.zeros_like(o_ref)

    @pl.when(tv_ref[i] != 0)
    def _():
        x = x_ref[...].astype(BF16)
        f = wg_ref.shape[-1]
        for n, c0 in enumerate(range(0, f, fchunk)):
            c1 = min(c0 + fchunk, f)
            g = _dot(x, wg_ref[0, :, c0:c1])
            u = _dot(x, wu_ref[0, :, c0:c1])
            y = _dot((_silu(g) * u).astype(BF16), wd_ref[0, c0:c1, :])
            if n == 0:
                o_ref[...] = y
            else:
                o_ref[...] += y


def _swiglu_tiles(x, w_gate, w_up, w_down, tile_expert, tile_valid, tm, fchunk, name):
    r, d = x.shape
    f = w_gate.shape[-1]
    once = pl.Buffered(1)
    grid_spec = pltpu.PrefetchScalarGridSpec(
        num_scalar_prefetch=2,
        grid=(r // tm,),
        in_specs=[pl.BlockSpec((tm, d), lambda i, te, tv: (i, 0)),
                  pl.BlockSpec((1, d, f), lambda i, te, tv: (te[i], 0, 0), pipeline_mode=once),
                  pl.BlockSpec((1, d, f), lambda i, te, tv: (te[i], 0, 0), pipeline_mode=once),
                  pl.BlockSpec((1, f, d), lambda i, te, tv: (te[i], 0, 0), pipeline_mode=once)],
        out_specs=pl.BlockSpec((tm, d), lambda i, te, tv: (i, 0)),
    )
    return pl.pallas_call(
        functools.partial(_swiglu_kernel, fchunk=fchunk),
        grid_spec=grid_spec,
        out_shape=jax.ShapeDtypeStruct((r, d), F32),
        compiler_params=_cp(("arbitrary",)),
        name=name,
    )(tile_expert, tile_valid, x, w_gate, w_up, w_down)


def _ple(h, p_ref, wp_ref, wgt_ref):
    gate = jax.nn.sigmoid(_dot(h.astype(BF16), wgt_ref[...]))
    proj = _dot(p_ref[...].astype(BF16), wp_ref[...])
    return h + proj * gate


def _post_dense_kernel(res_ref, y_ref, p_ref, g_ref, b_ref, wp_ref, wgt_ref, o_ref, *, alpha):
    h = _layer_norm(alpha * res_ref[...] + y_ref[...], g_ref[...], b_ref[...])
    o_ref[...] = _ple(h, p_ref, wp_ref, wgt_ref)


def _post_dense(res, y, p, g, b, w_proj, w_gate, alpha, tm):
    t, d = res.shape
    pd = p.shape[1]
    row = lambda i: (i, 0)
    fixed = lambda i: (0, 0)
    return pl.pallas_call(
        functools.partial(_post_dense_kernel, alpha=alpha),
        grid=(t // tm,),
        in_specs=[pl.BlockSpec((tm, d), row), pl.BlockSpec((tm, d), row),
                  pl.BlockSpec((tm, pd), row),
                  pl.BlockSpec((1, d), fixed), pl.BlockSpec((1, d), fixed),
                  pl.BlockSpec((pd, d), fixed), pl.BlockSpec((d, d), fixed)],
        out_specs=pl.BlockSpec((tm, d), row),
        out_shape=jax.ShapeDtypeStruct((t, d), F32),
        compiler_params=_cp(("parallel",)),
        name="post_ffn_dense",
    )(res, y, p, g.reshape(1, d), b.reshape(1, d), w_proj, w_gate)


def _row_copy(src_hbm, row, dst, r, sem):
    return pltpu.make_async_copy(src_hbm.at[pl.ds(row, 1), :], dst.at[pl.ds(r, 1), :], sem)


def _post_moe_kernel(slot_ref, res_ref, y_hbm, gw_ref, p_ref, g_ref, b_ref, wp_ref, wgt_ref,
                     o_ref, buf_ref, sem, *, alpha):
    tm = res_ref.shape[0]

    def start(r, _):
        for k in range(TOP_K):
            _row_copy(y_hbm, slot_ref[0, k, r], buf_ref.at[k], r, sem.at[k]).start()
        return 0

    def wait(r, _):
        for k in range(TOP_K):
            _row_copy(y_hbm, slot_ref[0, k, r], buf_ref.at[k], r, sem.at[k]).wait()
        return 0

    lax.fori_loop(0, tm, start, 0, unroll=8)
    lax.fori_loop(0, tm, wait, 0, unroll=8)
    gw = gw_ref[...]
    ffn = gw[:, 0:1] * buf_ref[0] + gw[:, 1:2] * buf_ref[1]
    h = _layer_norm(alpha * res_ref[...] + ffn, g_ref[...], b_ref[...])
    o_ref[...] = _ple(h, p_ref, wp_ref, wgt_ref)


def _post_moe(res, y_slots, slots, gate_w, p, g, b, w_proj, w_gate, alpha, tm):
    t, d = res.shape
    pd = p.shape[1]
    row = lambda i: (i, 0)
    fixed = lambda i: (0, 0)
    slots3 = slots.reshape(TOP_K, t // tm, tm).transpose(1, 0, 2)
    return pl.pallas_call(
        functools.partial(_post_moe_kernel, alpha=alpha),
        grid=(t // tm,),
        in_specs=[pl.BlockSpec((1, TOP_K, tm), lambda i: (i, 0, 0), memory_space=pltpu.SMEM),
                  pl.BlockSpec((tm, d), row),
                  pl.BlockSpec(memory_space=pl.ANY),
                  pl.BlockSpec((tm, TOP_K), row),
                  pl.BlockSpec((tm, pd), row),
                  pl.BlockSpec((1, d), fixed), pl.BlockSpec((1, d), fixed),
                  pl.BlockSpec((pd, d), fixed), pl.BlockSpec((d, d), fixed)],
        out_specs=pl.BlockSpec((tm, d), row),
        out_shape=jax.ShapeDtypeStruct((t, d), F32),
        scratch_shapes=[pltpu.VMEM((TOP_K, tm, d), F32), pltpu.SemaphoreType.DMA((TOP_K,))],
        compiler_params=_cp(("arbitrary",)),
        name="post_ffn_moe_combine",
    )(slots3, res, y_slots, gate_w, p, g.reshape(1, d), b.reshape(1, d), w_proj, w_gate)


def _inproj_kernel(x_ref, w_ref, o_ref):
    o_ref[...] = _dot(x_ref[...].astype(BF16), w_ref[...]).astype(o_ref.dtype)


def _inproj_conv_kernel(x_ref, w_ref, cw_ref, cb_ref, o_ref, ext_ref, carry_ref, *, tiles_per_seq):
    i = pl.program_id(0)
    j = pl.program_id(1)
    tm = x_ref.shape[0]
    halo = carry_ref.shape[1]

    @pl.when(i % tiles_per_seq == 0)
    def _():
        carry_ref[j] = jnp.zeros(carry_ref.shape[1:], F32)

    acc = _dot(x_ref[...].astype(BF16), w_ref[...])
    ext_ref[0:halo, :] = carry_ref[j]
    ext_ref[halo:halo + tm, :] = acc
    cw = cw_ref[...]
    out = cb_ref[...] + cw[0:1, :] * ext_ref[halo - 3:halo - 3 + tm, :]
    for k in range(1, SSD_CONV):
        out = out + cw[k:k + 1, :] * ext_ref[halo - 3 + k:halo - 3 + k + tm, :]
    carry_ref[j] = acc[tm - halo:tm, :]
    o_ref[...] = _silu(out).astype(o_ref.dtype)


def _ssd_inproj(x, w, tm, tn, name):
    t, d = x.shape
    n = w.shape[1]
    return pl.pallas_call(
        _inproj_kernel,
        grid=(t // tm, n // tn),
        in_specs=[pl.BlockSpec((tm, d), lambda i, j: (i, 0)),
                  pl.BlockSpec((d, tn), lambda i, j: (0, j))],
        out_specs=pl.BlockSpec((tm, tn), lambda i, j: (i, j)),
        out_shape=jax.ShapeDtypeStruct((t, n), BF16),
        compiler_params=_cp(("parallel", "arbitrary")),
        name=name,
    )(x, w)


def _ssd_inproj_conv(x, w, conv_w, conv_b, seq, tm, tn):
    t, d = x.shape
    n = w.shape[1]
    halo = 8
    return pl.pallas_call(
        functools.partial(_inproj_conv_kernel, tiles_per_seq=seq // tm),
        grid=(t // tm, n // tn),
        in_specs=[pl.BlockSpec((tm, d), lambda i, j: (i, 0)),
                  pl.BlockSpec((d, tn), lambda i, j: (0, j)),
                  pl.BlockSpec((SSD_CONV, tn), lambda i, j: (0, j)),
                  pl.BlockSpec((1, tn), lambda i, j: (0, j))],
        out_specs=pl.BlockSpec((tm, tn), lambda i, j: (i, j)),
        out_shape=jax.ShapeDtypeStruct((t, n), BF16),
        scratch_shapes=[pltpu.VMEM((tm + halo, tn), F32), pltpu.VMEM((n // tn, halo, tn), F32)],
        compiler_params=_cp(("arbitrary", "arbitrary")),
        name="ssd_inproj_conv_silu",
    )(x, w, conv_w, conv_b.reshape(1, n))


def _ssd_dt_kernel(x_ref, wh_ref, wl_ref, bias_ref, alog_ref, dt_ref, cs_ref):
    dt = _softplus(_dot_x2w2(x_ref[...], wh_ref[...], wl_ref[...]) + bias_ref[...])
    da = dt * (-jnp.exp(alog_ref[...]))
    ts = da.shape[0]
    r = lax.broadcasted_iota(I32, (ts, ts), 0)
    c = lax.broadcasted_iota(I32, (ts, ts), 1)
    same_chunk = (r // SSD_CHUNK) == (c // SSD_CHUNK)
    tri = jnp.where(jnp.logical_and(r >= c, same_chunk), 1.0, 0.0).astype(BF16)
    hi, mid, lo = _split3(da)
    dt_ref[...] = dt
    cs_ref[...] = _dot(tri, hi) + _dot(tri, mid) + _dot(tri, lo)


def _ssd_dt(x, w_dt, dt_bias, a_log, ts):
    t, d = x.shape
    nh = w_dt.shape[1]
    w_pad = jnp.zeros((d, LANES), F32).at[:, :nh].set(w_dt)
    wh, wl = _split2(w_pad)
    bias = jnp.zeros((1, LANES), F32).at[0, :nh].set(dt_bias)
    alog = jnp.zeros((1, LANES), F32).at[0, :nh].set(a_log)
    fixed = lambda i: (0, 0)
    row = lambda i: (i, 0)
    return pl.pallas_call(
        _ssd_dt_kernel,
        grid=(t // ts,),
        in_specs=[pl.BlockSpec((ts, d), row), pl.BlockSpec((d, LANES), fixed),
                  pl.BlockSpec((d, LANES), fixed), pl.BlockSpec((1, LANES), fixed),
                  pl.BlockSpec((1, LANES), fixed)],
        out_specs=[pl.BlockSpec((ts, LANES), row), pl.BlockSpec((ts, LANES), row)],
        out_shape=[jax.ShapeDtypeStruct((t, LANES), F32), jax.ShapeDtypeStruct((t, LANES), F32)],
        compiler_params=_cp(("parallel",)),
        name="ssd_dt_prep",
    )(x, wh, wl, bias, alog)


def _ssd_scan_kernel(xs_ref, bm_ref, cm_ref, z_ref, dt_ref, cs_ref, cst_ref, d_ref, ng_ref,
                     o_ref, h_ref):
    @pl.when(pl.program_id(2) == 0)
    def _():
        h_ref[...] = jnp.zeros_like(h_ref)

    lb, gw = xs_ref.shape
    hpg = gw // HEAD_DIM
    cl = SSD_CHUNK
    lane_head = lax.broadcasted_iota(I32, (cl, gw), 1) // HEAD_DIM
    row_head = lax.broadcasted_iota(I32, (1, gw), 1) // HEAD_DIM
    r = lax.broadcasted_iota(I32, (cl, cl), 0)
    c = lax.broadcasted_iota(I32, (cl, cl), 1)
    causal = r >= c

    def expand(cols, heads):
        out = cols[:, hpg - 1:hpg]
        for hh in range(hpg - 2, -1, -1):
            out = jnp.where(heads == hh, cols[:, hh:hh + 1], out)
        return out

    for ci in range(lb // cl):
        sl = slice(ci * cl, (ci + 1) * cl)
        x = xs_ref[sl, :].astype(F32)
        bm = bm_ref[sl, :]
        cm = cm_ref[sl, :]
        dt = dt_ref[0, sl, :]
        cs = cs_ref[0, sl, :]
        cst = cst_ref[0, :, sl]
        cs_e = expand(cs, lane_head)
        end_e = expand(cs[cl - 1:cl, :], row_head)
        xdt = x * expand(dt, lane_head)
        cb = _dot_nt(cm, bm)
        ms = []
        for hh in range(hpg):
            seg = cs[:, hh:hh + 1] - cst[hh:hh + 1, :]
            decay = jnp.exp(jnp.where(causal, seg, -jnp.inf))
            ms.append((cb * decay).astype(BF16))
        y_all = _dot(jnp.concatenate(ms, axis=0), xdt.astype(BF16))
        y = y_all[(hpg - 1) * cl:hpg * cl, :]
        for hh in range(hpg - 2, -1, -1):
            y = jnp.where(lane_head == hh, y_all[hh * cl:(hh + 1) * cl, :], y)
        h_in = h_ref[...]
        y = y + _dot(cm, h_in.astype(BF16)) * jnp.exp(cs_e)
        y = y + d_ref[0] * x
        states = _dot_tn(bm, (xdt * jnp.exp(end_e - cs_e)).astype(BF16))
        h_ref[...] = jnp.exp(end_e) * h_in + states
        y = y * _silu(z_ref[sl, :].astype(F32))
        y = y * lax.rsqrt(jnp.mean(y * y, axis=-1, keepdims=True) + RMS_EPS)
        o_ref[sl, :] = (y * ng_ref[...]).astype(o_ref.dtype)


def _ssd_scan(xbc, z, dt, cs, d_skip, norm_g, bsz, seq, lb):
    t = xbc.shape[0]
    d_inner = z.shape[1]
    g = SSD_GROUPS
    gw = d_inner // g
    hpg = gw // HEAD_DIM
    n = SSD_STATE
    nsb = seq // lb
    dt_g = dt.reshape(t, g, hpg).transpose(1, 0, 2)
    cs_g = cs.reshape(t, g, hpg).transpose(1, 0, 2)
    cs_t = cs.reshape(t, g, hpg).transpose(1, 2, 0)
    d_e = jnp.repeat(d_skip.astype(F32), HEAD_DIM).reshape(g, 1, gw)
    rows = lambda b, gi, s: b * nsb + s
    return pl.pallas_call(
        _ssd_scan_kernel,
        grid=(bsz, g, nsb),
        in_specs=[pl.BlockSpec((lb, gw), lambda b, gi, s: (rows(b, gi, s), gi)),
                  pl.BlockSpec((lb, n), lambda b, gi, s: (rows(b, gi, s), d_inner // n + gi)),
                  pl.BlockSpec((lb, n), lambda b, gi, s: (rows(b, gi, s), d_inner // n + g + gi)),
                  pl.BlockSpec((lb, gw), lambda b, gi, s: (rows(b, gi, s), gi)),
                  pl.BlockSpec((1, lb, hpg), lambda b, gi, s: (gi, rows(b, gi, s), 0)),
                  pl.BlockSpec((1, lb, hpg), lambda b, gi, s: (gi, rows(b, gi, s), 0)),
                  pl.BlockSpec((1, hpg, lb), lambda b, gi, s: (gi, 0, rows(b, gi, s))),
                  pl.BlockSpec((1, 1, gw), lambda b, gi, s: (gi, 0, 0)),
                  pl.BlockSpec((1, gw), lambda b, gi, s: (0, gi))],
        out_specs=pl.BlockSpec((lb, gw), lambda b, gi, s: (rows(b, gi, s), gi)),
        out_shape=jax.ShapeDtypeStruct((t, d_inner), BF16),
        scratch_shapes=[pltpu.VMEM((n, gw), F32)],
        compiler_params=_cp(("parallel", "parallel", "arbitrary")),
        name="ssd_chunk_scan",
    )(xbc, xbc, xbc, z, dt_g, cs_g, cs_t, d_e, norm_g.reshape(1, d_inner))


def _route_kernel(x_ref, rh_ref, rl_ref, idx_ref, gw_ref, rank_ref, cnt_ref, carry_ref):
    @pl.when(pl.program_id(0) == 0)
    def _():
        carry_ref[...] = jnp.zeros_like(carry_ref)

    xh, xl = _split2(x_ref[...])
    rh = rh_ref[...]
    logits = _dot_nt(rh, xh) + _dot_nt(rh, xl) + _dot_nt(rl_ref[...], xh)
    ne, tm = logits.shape
    e = lax.broadcasted_iota(I32, (ne, tm), 0)
    m1 = jnp.max(logits, axis=0, keepdims=True)
    i1 = jnp.min(jnp.where(logits == m1, e, ne), axis=0, keepdims=True)
    rest = jnp.where(e == i1, -jnp.inf, logits)
    m2 = jnp.max(rest, axis=0, keepdims=True)
    i2 = jnp.min(jnp.where(rest == m2, e, ne), axis=0, keepdims=True)
    ex = jnp.exp(m2 - m1)
    w1 = 1.0 / (1.0 + ex)
    w2 = ex / (1.0 + ex)
    sel1 = e == i1
    sel2 = e == i2
    sel = jnp.where(jnp.logical_or(sel1, sel2), 1.0, 0.0)
    r = lax.broadcasted_iota(I32, (tm, tm), 0)
    c = lax.broadcasted_iota(I32, (tm, tm), 1)
    before = jnp.where(r < c, 1.0, 0.0).astype(BF16)
    base = carry_ref[...][:, 0:1]
    rank = _dot(sel.astype(BF16), before) + base
    r1 = jnp.sum(jnp.where(sel1, rank, 0.0), axis=0, keepdims=True)
    r2 = jnp.sum(jnp.where(sel2, rank, 0.0), axis=0, keepdims=True)
    idx_ref[...] = jnp.concatenate([i1, i2], axis=0)
    gw_ref[...] = jnp.concatenate([w1, w2], axis=0)
    rank_ref[...] = jnp.concatenate([r1, r2], axis=0).astype(I32)
    total = carry_ref[...] + jnp.sum(sel, axis=1, keepdims=True)
    carry_ref[...] = total
    cnt_ref[...] = total.astype(I32)


def _moe_route(x, router, tm):
    t, d = x.shape
    ne = router.shape[1]
    rh, rl = _split2(router.T)
    fixed = lambda i: (0, 0)
    col = lambda i: (0, i)
    return pl.pallas_call(
        _route_kernel,
        grid=(t // tm,),
        in_specs=[pl.BlockSpec((tm, d), lambda i: (i, 0)),
                  pl.BlockSpec((ne, d), fixed), pl.BlockSpec((ne, d), fixed)],
        out_specs=[pl.BlockSpec((TOP_K, tm), col), pl.BlockSpec((TOP_K, tm), col),
                   pl.BlockSpec((TOP_K, tm), col), pl.BlockSpec((ne, LANES), fixed)],
        out_shape=[jax.ShapeDtypeStruct((TOP_K, t), I32), jax.ShapeDtypeStruct((TOP_K, t), F32),
                   jax.ShapeDtypeStruct((TOP_K, t), I32), jax.ShapeDtypeStruct((ne, LANES), I32)],
        scratch_shapes=[pltpu.VMEM((ne, LANES), F32)],
        compiler_params=_cp(("arbitrary",)),
        name="moe_route_top2",
    )(x, rh, rl)


def _gather_rows_kernel(tok_ref, x_hbm, o_ref, buf_ref, sem):
    tm = o_ref.shape[0]

    def start(r, _):
        _row_copy(x_hbm, tok_ref[0, 0, r], buf_ref, r, sem).start()
        return 0

    def wait(r, _):
        _row_copy(x_hbm, tok_ref[0, 0, r], buf_ref, r, sem).wait()
        return 0

    lax.fori_loop(0, tm, start, 0, unroll=8)
    lax.fori_loop(0, tm, wait, 0, unroll=8)
    o_ref[...] = buf_ref[...].astype(o_ref.dtype)


def _gather_rows(x, token_of_slot, tm):
    d = x.shape[1]
    ns = token_of_slot.shape[0]
    return pl.pallas_call(
        _gather_rows_kernel,
        grid=(ns // tm,),
        in_specs=[pl.BlockSpec((1, 1, tm), lambda i: (i, 0, 0), memory_space=pltpu.SMEM),
                  pl.BlockSpec(memory_space=pl.ANY)],
        out_specs=pl.BlockSpec((tm, d), lambda i: (i, 0)),
        out_shape=jax.ShapeDtypeStruct((ns, d), BF16),
        scratch_shapes=[pltpu.VMEM((tm, d), F32), pltpu.SemaphoreType.DMA(())],
        compiler_params=_cp(("arbitrary",)),
        name="moe_dispatch_gather",
    )(token_of_slot.reshape(ns // tm, 1, tm), x)


def _group_plan(idx, rank, counts, tm, n_tiles):
    ne = counts.shape[0]
    t = idx.shape[1]
    tiles_e = (counts + tm - 1) // tm
    tile_end = jnp.cumsum(tiles_e)
    tile_start = tile_end - tiles_e
    start_of = jnp.sum(jnp.where(idx[:, :, None] == jnp.arange(ne, dtype=I32), tile_start, 0), axis=-1)
    slots = start_of * tm + rank
    tile_ids = jnp.arange(n_tiles, dtype=I32)
    tile_expert = jnp.minimum(jnp.searchsorted(tile_end, tile_ids, side="right"), ne - 1).astype(I32)
    tile_valid = (tile_ids < tile_end[ne - 1]).astype(I32)
    tokens = jnp.broadcast_to(jnp.arange(t, dtype=I32), (TOP_K, t))
    token_of_slot = jnp.zeros((n_tiles * tm,), I32).at[slots.reshape(-1)].set(tokens.reshape(-1))
    return slots.astype(I32), tile_expert, tile_valid, token_of_slot


def _pick(pref, n):
    return pref if n % pref == 0 else n


def kernel(x, p, ln_mix_g, ln_mix_b, ln_ffn_g, ln_ffn_b, fox_w_in, fox_b_f, fox_w_o, ssd_w_in, ssd_conv_w, ssd_conv_b, ssd_dt_bias, ssd_a_log, ssd_d, ssd_norm_g, ssd_w_out, ffn_w_gate, ffn_w_up, ffn_w_down, moe_router, moe_w_gate, moe_w_up, moe_w_down, ple_w_proj, ple_w_gate):
    bsz, seq, d = x.shape
    depth = p.shape[0]
    t = bsz * seq
    alpha = (2.0 * depth) ** 0.25
    x2 = x.reshape(t, d)
    p2 = p.reshape(depth, t, p.shape[-1])
    tm_big = _pick(1024, t)
    tm = _pick(512, t)
    ts = _pick(512, seq)
    one_tile = jnp.zeros((t // tm,), I32)
    all_valid = jnp.ones((t // tm,), I32)

    w_in = fox_w_in[0]
    qkv = _qkv_proj(x2, w_in[:, :3 * d].astype(BF16), tm_big)
    cum = _fgate_cumlog(x2, w_in[:, 3 * d:], fox_b_f[0], bsz, seq, ts)
    attn = _fox_attention(qkv, cum, bsz, seq, ts)
    h = _proj_residual_ln(attn, fox_w_o[0].astype(BF16), x2, ln_mix_g[0], ln_mix_b[0], alpha, tm,
                          "fox_out_proj_ln")
    y = _swiglu_tiles(h, ffn_w_gate.astype(BF16), ffn_w_up.astype(BF16), ffn_w_down.astype(BF16),
                      one_tile, all_valid, tm, 256, "ffn_swiglu_dense")
    h = _post_dense(h, y, p2[0], ln_ffn_g[0], ln_ffn_b[0], ple_w_proj[0].astype(BF16),
                    ple_w_gate[0].astype(BF16), alpha, tm)

    w_in = ssd_w_in[0]
    d_inner = ssd_norm_g.shape[1]
    conv_ch = ssd_conv_w.shape[2]
    z = _ssd_inproj(h, w_in[:, :d_inner].astype(BF16), tm_big, _pick(1024, d_inner), "ssd_inproj_z")
    xbc = _ssd_inproj_conv(h, w_in[:, d_inner:d_inner + conv_ch].astype(BF16), ssd_conv_w[0],
                           ssd_conv_b[0], seq, _pick(1024, seq), _pick(1024, conv_ch))
    dt, cs = _ssd_dt(h, w_in[:, d_inner + conv_ch:], ssd_dt_bias[0], ssd_a_log[0], ts)
    nh = ssd_dt_bias.shape[1]
    ymix = _ssd_scan(xbc, z, dt[:, :nh], cs[:, :nh], ssd_d[0], ssd_norm_g[0], bsz, seq, ts)
    h = _proj_residual_ln(ymix, ssd_w_out[0].astype(BF16), h, ln_mix_g[1], ln_mix_b[1], alpha, tm,
                          "ssd_out_proj_ln")
    ne = moe_router.shape[2]
    idx, gate_w, rank, counts = _moe_route(h, moe_router[0], tm)
    n_tiles = (TOP_K * t) // tm + ne
    slots, tile_expert, tile_valid, token_of_slot = _group_plan(idx, rank, counts[:, 0], tm, n_tiles)
    xs = _gather_rows(h, token_of_slot, tm)
    ys = _swiglu_tiles(xs, moe_w_gate[0].astype(BF16), moe_w_up[0].astype(BF16),
                       moe_w_down[0].astype(BF16), tile_expert, tile_valid, tm, 512,
                       "moe_swiglu_experts")
    tmc = _pick(256, t)
    h = _post_moe(h, ys, slots, gate_w.T, p2[1], ln_ffn_g[1], ln_ffn_b[1],
                  ple_w_proj[1].astype(BF16), ple_w_gate[1].astype(BF16), alpha, tmc)
    return h.reshape(bsz, seq, d)
```

```python
import functools

import jax
import jax.numpy as jnp
from jax import lax
from jax.experimental import pallas as pl
from jax.experimental.pallas import tpu as pltpu

F32 = jnp.float32
BF16 = jnp.bfloat16
I32 = jnp.int32

LN_EPS = 1e-5
RMS_EPS = 1e-5
LANES = 128
HEAD_DIM = 64
SSD_GROUPS = 8
SSD_STATE = 128
SSD_CONV = 4
SSD_CHUNK = 128
TOP_K = 2
LOG2E = 1.4426950408889634
VMEM_LIMIT = 56 * 1024 * 1024


def _cp(sem, vmem=VMEM_LIMIT):
    return pltpu.CompilerParams(dimension_semantics=sem, vmem_limit_bytes=vmem)


def _dot(a, b):
    return jnp.dot(a, b, preferred_element_type=F32)


def _dot_nt(a, b):
    return lax.dot_general(a, b, (((1,), (1,)), ((), ())), preferred_element_type=F32)


def _dot_tn(a, b):
    return lax.dot_general(a, b, (((0,), (0,)), ((), ())), preferred_element_type=F32)


def _split2(v):
    hi = v.astype(BF16)
    lo = (v - hi.astype(F32)).astype(BF16)
    return hi, lo


def _split3(v):
    hi = v.astype(BF16)
    r = v - hi.astype(F32)
    mid = r.astype(BF16)
    lo = (r - mid.astype(F32)).astype(BF16)
    return hi, mid, lo


def _dot_x2w2(x, wh, wl):
    xh, xl = _split2(x)
    return _dot(xh, wh) + _dot(xl, wh) + _dot(xh, wl)


def _layer_norm(v, g, b):
    mu = jnp.mean(v, axis=-1, keepdims=True)
    vc = v - mu
    var = jnp.mean(vc * vc, axis=-1, keepdims=True)
    return vc * lax.rsqrt(var + LN_EPS) * g + b


def _silu(v):
    return v * jax.nn.sigmoid(v)


def _softplus(v):
    return jnp.maximum(v, 0.0) + jnp.log1p(jnp.exp(-jnp.abs(v)))


def _qkv_kernel(x_ref, w_ref, o_ref, *, scale):
    j = pl.program_id(1)
    acc = _dot(x_ref[...].astype(BF16), w_ref[...])
    acc = acc * jnp.where(j == 0, scale, 1.0).astype(F32)
    for c in range(o_ref.shape[0]):
        o_ref[c] = acc[:, c * LANES:(c + 1) * LANES].astype(BF16)


def _qkv_proj(x, w_qkv, tm):
    t, d = x.shape
    n = w_qkv.shape[1]
    tn = d
    cpb = tn // LANES
    return pl.pallas_call(
        functools.partial(_qkv_kernel, scale=LOG2E * HEAD_DIM ** -0.5),
        grid=(t // tm, n // tn),
        in_specs=[pl.BlockSpec((tm, d), lambda i, j: (i, 0)),
                  pl.BlockSpec((d, tn), lambda i, j: (0, j))],
        out_specs=pl.BlockSpec((cpb, tm, LANES), lambda i, j: (j, i, 0)),
        out_shape=jax.ShapeDtypeStruct((n // LANES, t, LANES), BF16),
        compiler_params=_cp(("parallel", "arbitrary")),
        name="fox_qkv_proj",
    )(x, w_qkv)


def _fgate_kernel(x_ref, wh_ref, wl_ref, b_ref, o_ref, carry_ref):
    @pl.when(pl.program_id(1) == 0)
    def _():
        carry_ref[...] = jnp.zeros_like(carry_ref)

    z = _dot_x2w2(x_ref[...], wh_ref[...], wl_ref[...]) + b_ref[...]
    log_f = jnp.minimum(z, 0.0) - jnp.log1p(jnp.exp(-jnp.abs(z)))
    ts = log_f.shape[0]
    r = lax.broadcasted_iota(I32, (ts, ts), 0)
    c = lax.broadcasted_iota(I32, (ts, ts), 1)
    tri = jnp.where(r >= c, 1.0, 0.0).astype(BF16)
    hi, mid, lo = _split3(log_f)
    cs = _dot(tri, hi) + _dot(tri, mid) + _dot(tri, lo) + carry_ref[...]
    o_ref[...] = cs
    carry_ref[...] = cs[ts - 1:ts, :]


def _fgate_cumlog(x, w_f, b_f, bsz, seq, ts):
    t, d = x.shape
    nh = w_f.shape[1]
    w_pad = jnp.zeros((d, LANES), F32).at[:, :nh].set(w_f)
    wh, wl = _split2(w_pad)
    b_pad = jnp.zeros((1, LANES), F32).at[0, :nh].set(b_f)
    nsb = seq // ts
    return pl.pallas_call(
        _fgate_kernel,
        grid=(bsz, nsb),
        in_specs=[pl.BlockSpec((ts, d), lambda b, s: (b * nsb + s, 0)),
                  pl.BlockSpec((d, LANES), lambda b, s: (0, 0)),
                  pl.BlockSpec((d, LANES), lambda b, s: (0, 0)),
                  pl.BlockSpec((1, LANES), lambda b, s: (0, 0))],
        out_specs=pl.BlockSpec((ts, LANES), lambda b, s: (b * nsb + s, 0)),
        out_shape=jax.ShapeDtypeStruct((t, LANES), F32),
        scratch_shapes=[pltpu.VMEM((1, LANES), F32)],
        compiler_params=_cp(("parallel", "arbitrary")),
        name="fox_forget_cumlog",
    )(x, wh, wl, b_pad)


def _pieces3(v):
    hi = v.astype(BF16).astype(F32)
    r = v - hi
    mid = r.astype(BF16).astype(F32)
    lo = (r - mid).astype(BF16).astype(F32)
    return hi, mid, lo


def _lane_columns(lane, base, cols):
    out = jnp.zeros(lane.shape, F32)
    for i, col in enumerate(cols):
        out = jnp.where(lane == base + i, col, out)
    return out


def _fox_kernel(q_ref, k_ref, v_ref, c_ref, o_ref, qa_ref, ka_ref, vt_ref, sa_ref, sb_ref, p_ref,
                m_ref, alpha_ref, acc_ref, *, tq, cb):
    qi = pl.program_id(2)
    seq = k_ref.shape[1]
    bases = (HEAD_DIM, 0)

    def data_lanes(lane, h):
        return (lane < HEAD_DIM) if h == 0 else (lane >= HEAD_DIM)

    @pl.when(qi == 0)
    def _():
        for c0 in range(0, seq, tq):
            rows = slice(c0, c0 + tq)
            k = k_ref[0, rows, :]
            v = v_ref[0, rows, :].astype(F32)
            c2 = c_ref[0, 0, rows, :] * LOG2E
            lane = lax.broadcasted_iota(I32, k.shape, 1)
            for h in range(2):
                hi, mid, lo = _pieces3(-c2[:, h:h + 1])
                k_aug = _lane_columns(lane, bases[h], [1.0, 1.0, 1.0, hi, mid, lo])
                v_aug = _lane_columns(lane, bases[h], [1.0])
                ka_ref[h, rows, :] = jnp.where(data_lanes(lane, h), k, k_aug.astype(BF16))
                vt_ref[h, :, rows] = jnp.where(data_lanes(lane, h), v, v_aug).T.astype(BF16)

    q = q_ref[0]
    lane = lax.broadcasted_iota(I32, q.shape, 1)
    cq2 = c_ref[0, 0, pl.ds(pl.multiple_of(qi * tq, tq), tq), :] * LOG2E
    for h in range(2):
        hi, mid, lo = _pieces3(cq2[:, h:h + 1])
        q_aug = _lane_columns(lane, bases[h], [hi, mid, lo, 1.0, 1.0, 1.0])
        qa_ref[h] = jnp.where(data_lanes(lane, h), q, q_aug.astype(BF16))
    m_ref[...] = jnp.full(m_ref.shape, -jnp.inf, F32)
    acc_ref[...] = jnp.zeros(acc_ref.shape, F32)

    def scores(j, s_ref):
        off = pl.multiple_of(j * tq, tq)
        for h in range(2):
            s_ref[h] = _dot_nt(ka_ref[h, pl.ds(off, tq), :], qa_ref[h])

    def softmax_pv(j, s_ref, diagonal):
        off = pl.multiple_of(j * tq, tq)
        for h in range(2):
            for c0 in range(0, tq, cb):
                cols = slice(c0, c0 + cb)
                nrow = c0 + cb if diagonal else tq
                s = s_ref[h, 0:nrow, cols]
                if diagonal:
                    key = lax.broadcasted_iota(I32, s.shape, 0)
                    qry = c0 + lax.broadcasted_iota(I32, s.shape, 1)
                    s = jnp.where(key <= qry, s, -jnp.inf)
                m_old = m_ref[h, :, cols]
                m_new = jnp.maximum(m_old, jnp.max(s, axis=0, keepdims=True))
                m_ref[h, :, cols] = m_new
                alpha_ref[h, :, cols] = jnp.exp2(m_old - m_new)
                p_ref[h, 0:nrow, cols] = jnp.exp2(s - m_new).astype(BF16)
                if nrow < tq:
                    p_ref[h, nrow:tq, cols] = jnp.zeros((tq - nrow, cb), BF16)
            acc_ref[h] = alpha_ref[h] * acc_ref[h] + _dot(vt_ref[h, :, pl.ds(off, tq)], p_ref[h])

    scores(0, sa_ref)

    def pair(i, carry):
        scores(2 * i + 1, sb_ref)
        softmax_pv(2 * i, sa_ref, False)
        scores(2 * i + 2, sa_ref)
        softmax_pv(2 * i + 1, sb_ref, False)
        return carry

    lax.fori_loop(0, qi // 2, pair, 0)

    @pl.when(qi % 2 == 0)
    def _():
        softmax_pv(qi, sa_ref, True)

    @pl.when(qi % 2 == 1)
    def _():
        scores(qi, sb_ref)
        softmax_pv(qi - 1, sa_ref, False)
        softmax_pv(qi, sb_ref, True)

    a0 = acc_ref[0]
    a1 = acc_ref[1]
    row = lax.broadcasted_iota(I32, a0.shape, 0)
    out_t = jnp.where(row < HEAD_DIM, a0 / a0[bases[0]:bases[0] + 1, :],
                      a1 / a1[bases[1]:bases[1] + 1, :])
    o_ref[...] = out_t.T.astype(o_ref.dtype)


def _fox_attention(qkv, cum, bsz, seq, tq):
    hp = qkv.shape[0] // 3
    t = qkv.shape[1]
    c4 = cum[:, :2 * hp].reshape(bsz, seq, hp, 2).transpose(0, 2, 1, 3)
    nq = seq // tq
    return pl.pallas_call(
        functools.partial(_fox_kernel, tq=tq, cb=LANES),
        grid=(bsz, hp, nq),
        in_specs=[pl.BlockSpec((1, tq, LANES), lambda b, h, i: (h, b * nq + i, 0)),
                  pl.BlockSpec((1, seq, LANES), lambda b, h, i: (hp + h, b, 0)),
                  pl.BlockSpec((1, seq, LANES), lambda b, h, i: (2 * hp + h, b, 0)),
                  pl.BlockSpec((1, 1, seq, 2), lambda b, h, i: (b, h, 0, 0))],
        out_specs=pl.BlockSpec((tq, LANES), lambda b, h, i: (b * nq + i, h)),
        out_shape=jax.ShapeDtypeStruct((t, hp * LANES), BF16),
        scratch_shapes=[pltpu.VMEM((2, tq, LANES), BF16),
                        pltpu.VMEM((2, seq, LANES), BF16),
                        pltpu.VMEM((2, LANES, seq), BF16),
                        pltpu.VMEM((2, tq, tq), F32),
                        pltpu.VMEM((2, tq, tq), F32),
                        pltpu.VMEM((2, tq, tq), BF16),
                        pltpu.VMEM((2, 1, tq), F32),
                        pltpu.VMEM((2, 1, tq), F32),
                        pltpu.VMEM((2, LANES, tq), F32)],
        compiler_params=_cp(("parallel", "parallel", "arbitrary")),
        name="fox_flash_attention",
    )(qkv, qkv, qkv, c4)


def _proj_ln_kernel(a_ref, w_ref, res_ref, g_ref, b_ref, o_ref, *, alpha):
    y = _dot(a_ref[...], w_ref[...])
    o_ref[...] = _layer_norm(alpha * res_ref[...] + y, g_ref[...], b_ref[...])


def _proj_residual_ln(a, w, res, g, b, alpha, tm, name):
    t, k = a.shape
    d = w.shape[1]
    return pl.pallas_call(
        functools.partial(_proj_ln_kernel, alpha=alpha),
        grid=(t // tm,),
        in_specs=[pl.BlockSpec((tm, k), lambda i: (i, 0)),
                  pl.BlockSpec((k, d), lambda i: (0, 0)),
                  pl.BlockSpec((tm, d), lambda i: (i, 0)),
                  pl.BlockSpec((1, d), lambda i: (0, 0)),
                  pl.BlockSpec((1, d), lambda i: (0, 0))],
        out_specs=pl.BlockSpec((tm, d), lambda i: (i, 0)),
        out_shape=jax.ShapeDtypeStruct((t, d), F32),
        compiler_params=_cp(("parallel",)),
        name=name,
    )(a, w, res, g.reshape(1, d), b.reshape(1, d))


def _swiglu_kernel(te_ref, tv_ref, x_ref, wg_ref, wu_ref, wd_ref, o_ref, *, fchunk):
    i = pl.program_id(0)

    @pl.when(tv_ref[i] == 0)
    def _():
        o_ref[...] = jnp.zeros_like(o_ref)

    @pl.when(tv_ref[i] != 0)
    def _():
        x = x_ref[...].astype(BF16)
        f = wg_ref.shape[-1]
        for n, c0 in enumerate(range(0, f, fchunk)):
            c1 = min(c0 + fchunk, f)
            g = _dot(x, wg_ref[0, :, c0:c1])
            u = _dot(x, wu_ref[0, :, c0:c1])
            y = _dot((_silu(g) * u).astype(BF16), wd_ref[0, c0:c1, :])
            if n == 0:
                o_ref[...] = y
            else:
                o_ref[...] += y


def _swiglu_tiles(x, w_gate, w_up, w_down, tile_expert, tile_valid, tm, fchunk, name):
    r, d = x.shape
    f = w_gate.shape[-1]
    once = pl.Buffered(1)
    grid_spec = pltpu.PrefetchScalarGridSpec(
        num_scalar_prefetch=2,
        grid=(r // tm,),
        in_specs=[pl.BlockSpec((tm, d), lambda i, te, tv: (i, 0)),
                  pl.BlockSpec((1, d, f), lambda i, te, tv: (te[i], 0, 0), pipeline_mode=once),
                  pl.BlockSpec((1, d, f), lambda i, te, tv: (te[i], 0, 0), pipeline_mode=once),
                  pl.BlockSpec((1, f, d), lambda i, te, tv: (te[i], 0, 0), pipeline_mode=once)],
        out_specs=pl.BlockSpec((tm, d), lambda i, te, tv: (i, 0)),
    )
    return pl.pallas_call(
        functools.partial(_swiglu_kernel, fchunk=fchunk),
        grid_spec=grid_spec,
        out_shape=jax.ShapeDtypeStruct((r, d), F32),
        compiler_params=_cp(("arbitrary",)),
        name=name,
    )(tile_expert, tile_valid, x, w_gate, w_up, w_down)


def _ple(h, p_ref, wp_ref, wgt_ref):
    gate = jax.nn.sigmoid(_dot(h.astype(BF16), wgt_ref[...]))
    proj = _dot(p_ref[...].astype(BF16), wp_ref[...])
    return h + proj * gate


def _post_dense_kernel(res_ref, y_ref, p_ref, g_ref, b_ref, wp_ref, wgt_ref, o_ref, *, alpha):
    h = _layer_norm(alpha * res_ref[...] + y_ref[...], g_ref[...], b_ref[...])
    o_ref[...] = _ple(h, p_ref, wp_ref, wgt_ref)


def _post_dense(res, y, p, g, b, w_proj, w_gate, alpha, tm):
    t, d = res.shape
    pd = p.shape[1]
    row = lambda i: (i, 0)
    fixed = lambda i: (0, 0)
    return pl.pallas_call(
        functools.partial(_post_dense_kernel, alpha=alpha),
        grid=(t // tm,),
        in_specs=[pl.BlockSpec((tm, d), row), pl.BlockSpec((tm, d), row),
                  pl.BlockSpec((tm, pd), row),
                  pl.BlockSpec((1, d), fixed), pl.BlockSpec((1, d), fixed),
                  pl.BlockSpec((pd, d), fixed), pl.BlockSpec((d, d), fixed)],
        out_specs=pl.BlockSpec((tm, d), row),
        out_shape=jax.ShapeDtypeStruct((t, d), F32),
        compiler_params=_cp(("parallel",)),
        name="post_ffn_dense",
    )(res, y, p, g.reshape(1, d), b.reshape(1, d), w_proj, w_gate)


def _row_copy(src_hbm, row, dst, r, sem):
    return pltpu.make_async_copy(src_hbm.at[pl.ds(row, 1), :], dst.at[pl.ds(r, 1), :], sem)


def _post_moe_kernel(res_ref, y1_ref, y2_ref, gw_ref, p_ref, g_ref, b_ref, wp_ref, wgt_ref,
                     o_ref, *, alpha):
    gw = gw_ref[...]
    ffn = gw[:, 0:1] * y1_ref[...] + gw[:, 1:2] * y2_ref[...]
    h = _layer_norm(alpha * res_ref[...] + ffn, g_ref[...], b_ref[...])
    o_ref[...] = _ple(h, p_ref, wp_ref, wgt_ref)


def _post_moe(res, y_tok, gate_w, p, g, b, w_proj, w_gate, alpha, tm):
    t, d = res.shape
    pd = p.shape[1]
    nt = t // tm
    row = lambda i: (i, 0)
    fixed = lambda i: (0, 0)
    return pl.pallas_call(
        functools.partial(_post_moe_kernel, alpha=alpha),
        grid=(nt,),
        in_specs=[pl.BlockSpec((tm, d), row),
                  pl.BlockSpec((tm, d), row),
                  pl.BlockSpec((tm, d), lambda i: (nt + i, 0)),
                  pl.BlockSpec((tm, TOP_K), row),
                  pl.BlockSpec((tm, pd), row),
                  pl.BlockSpec((1, d), fixed), pl.BlockSpec((1, d), fixed),
                  pl.BlockSpec((pd, d), fixed), pl.BlockSpec((d, d), fixed)],
        out_specs=pl.BlockSpec((tm, d), row),
        out_shape=jax.ShapeDtypeStruct((t, d), F32),
        compiler_params=_cp(("parallel",)),
        name="post_ffn_moe_combine",
    )(res, y_tok, y_tok, gate_w, p, g.reshape(1, d), b.reshape(1, d), w_proj, w_gate)


def _moe_experts_kernel(te_ref, src_ref, nxt_ref, dst_ref, x_hbm, wg_ref, wu_ref, wd_ref, y_hbm,
                        xbuf0, xbuf1, ybuf0, ybuf1, xb_ref, gsem, ssem, *, fchunk):
    i = pl.program_id(0)
    last = pl.num_programs(0) - 1
    tm = xbuf0.shape[0]
    xbufs = (xbuf0, xbuf1)
    ybufs = (ybuf0, ybuf1)

    def gather(row, buf, r, s):
        return _row_copy(x_hbm, row, buf, r, gsem.at[s])

    def scatter(buf, r, row, s):
        return pltpu.make_async_copy(buf.at[pl.ds(r, 1), :], y_hbm.at[pl.ds(row, 1), :], ssem.at[s])

    def wait_all(make):
        def body(r, _):
            make(r).wait()
            return 0
        lax.fori_loop(0, tm, body, 0, unroll=8)

    @pl.when(i == 0)
    def _():
        def body(r, _):
            gather(src_ref[0, 0, r], xbuf0, r, 0).start()
            return 0
        lax.fori_loop(0, tm, body, 0, unroll=8)
        ybuf1[...] = jnp.zeros_like(ybuf1)

    def step(s):
        o = 1 - s
        wait_all(lambda r: gather(0, xbufs[s], r, s))

        @pl.when(i >= 1)
        def _():
            wait_all(lambda r: scatter(ybufs[s], r, 0, s))

        for r in range(tm):
            gather(nxt_ref[0, 0, r], xbufs[o], r, o).start()
        for r in range(tm):
            scatter(ybufs[o], r, dst_ref[0, 0, r], o).start()
        xb_ref[...] = xbufs[s][...].astype(BF16)
        f = wg_ref.shape[-1]
        for n, c0 in enumerate(range(0, f, fchunk)):
            c1 = min(c0 + fchunk, f)
            g = _dot(xb_ref[...], wg_ref[0, :, c0:c1])
            u = _dot(xb_ref[...], wu_ref[0, :, c0:c1])
            y = _dot((_silu(g) * u).astype(BF16), wd_ref[0, c0:c1, :])
            if n == 0:
                ybufs[s][...] = y
            else:
                ybufs[s][...] += y

        @pl.when(i == last)
        def _():
            wait_all(lambda r: gather(0, xbufs[o], r, o))
            wait_all(lambda r: scatter(ybufs[o], r, 0, o))

    for s in range(2):
        pl.when(i % 2 == s)(functools.partial(step, s))


def _moe_experts(x, w_gate, w_up, w_down, tile_expert, src_tab, dst_tab, n_out_rows, tm, fchunk):
    d = x.shape[1]
    f = w_gate.shape[-1]
    steps = tile_expert.shape[0]
    once = pl.Buffered(1)
    smem_tile = lambda index_map: pl.BlockSpec((1, 1, tm), index_map, memory_space=pltpu.SMEM)
    grid_spec = pltpu.PrefetchScalarGridSpec(
        num_scalar_prefetch=1,
        grid=(steps,),
        in_specs=[smem_tile(lambda i, te: (i, 0, 0)),
                  smem_tile(lambda i, te: (jnp.minimum(i + 1, steps - 1), 0, 0)),
                  smem_tile(lambda i, te: (i, 0, 0)),
                  pl.BlockSpec(memory_space=pl.ANY),
                  pl.BlockSpec((1, d, f), lambda i, te: (te[i], 0, 0), pipeline_mode=once),
                  pl.BlockSpec((1, d, f), lambda i, te: (te[i], 0, 0), pipeline_mode=once),
                  pl.BlockSpec((1, f, d), lambda i, te: (te[i], 0, 0), pipeline_mode=once)],
        out_specs=pl.BlockSpec(memory_space=pl.ANY),
        scratch_shapes=[pltpu.VMEM((tm, d), F32), pltpu.VMEM((tm, d), F32),
                        pltpu.VMEM((tm, d), F32), pltpu.VMEM((tm, d), F32),
                        pltpu.VMEM((tm, d), BF16),
                        pltpu.SemaphoreType.DMA((2,)), pltpu.SemaphoreType.DMA((2,))],
    )
    return pl.pallas_call(
        functools.partial(_moe_experts_kernel, fchunk=fchunk),
        grid_spec=grid_spec,
        out_shape=jax.ShapeDtypeStruct((n_out_rows, d), F32),
        compiler_params=_cp(("arbitrary",)),
        name="moe_experts_fused_dispatch",
    )(tile_expert, src_tab, src_tab, dst_tab, x, w_gate, w_up, w_down)


def _inproj_kernel(x_ref, w_ref, o_ref):
    o_ref[...] = _dot(x_ref[...].astype(BF16), w_ref[...]).astype(o_ref.dtype)


def _inproj_conv_kernel(x_ref, w_ref, cw_ref, cb_ref, o_ref, ext_ref, carry_ref, *, tiles_per_seq):
    i = pl.program_id(0)
    j = pl.program_id(1)
    tm = x_ref.shape[0]
    halo = carry_ref.shape[1]

    @pl.when(i % tiles_per_seq == 0)
    def _():
        carry_ref[j] = jnp.zeros(carry_ref.shape[1:], F32)

    acc = _dot(x_ref[...].astype(BF16), w_ref[...])
    ext_ref[0:halo, :] = carry_ref[j]
    ext_ref[halo:halo + tm, :] = acc
    cw = cw_ref[...]
    out = cb_ref[...] + cw[0:1, :] * ext_ref[halo - 3:halo - 3 + tm, :]
    for k in range(1, SSD_CONV):
        out = out + cw[k:k + 1, :] * ext_ref[halo - 3 + k:halo - 3 + k + tm, :]
    carry_ref[j] = acc[tm - halo:tm, :]
    o_ref[...] = _silu(out).astype(o_ref.dtype)


def _ssd_inproj(x, w, tm, tn, name):
    t, d = x.shape
    n = w.shape[1]
    return pl.pallas_call(
        _inproj_kernel,
        grid=(t // tm, n // tn),
        in_specs=[pl.BlockSpec((tm, d), lambda i, j: (i, 0)),
                  pl.BlockSpec((d, tn), lambda i, j: (0, j))],
        out_specs=pl.BlockSpec((tm, tn), lambda i, j: (i, j)),
        out_shape=jax.ShapeDtypeStruct((t, n), BF16),
        compiler_params=_cp(("parallel", "arbitrary")),
        name=name,
    )(x, w)


def _ssd_inproj_conv(x, w, conv_w, conv_b, seq, tm, tn):
    t, d = x.shape
    n = w.shape[1]
    halo = 8
    return pl.pallas_call(
        functools.partial(_inproj_conv_kernel, tiles_per_seq=seq // tm),
        grid=(t // tm, n // tn),
        in_specs=[pl.BlockSpec((tm, d), lambda i, j: (i, 0)),
                  pl.BlockSpec((d, tn), lambda i, j: (0, j)),
                  pl.BlockSpec((SSD_CONV, tn), lambda i, j: (0, j)),
                  pl.BlockSpec((1, tn), lambda i, j: (0, j))],
        out_specs=pl.BlockSpec((tm, tn), lambda i, j: (i, j)),
        out_shape=jax.ShapeDtypeStruct((t, n), BF16),
        scratch_shapes=[pltpu.VMEM((tm + halo, tn), F32), pltpu.VMEM((n // tn, halo, tn), F32)],
        compiler_params=_cp(("arbitrary", "arbitrary")),
        name="ssd_inproj_conv_silu",
    )(x, w, conv_w, conv_b.reshape(1, n))


def _ssd_dt_kernel(x_ref, wh_ref, wl_ref, bias_ref, alog_ref, dt_ref, cs_ref):
    dt = _softplus(_dot_x2w2(x_ref[...], wh_ref[...], wl_ref[...]) + bias_ref[...])
    da = dt * (-jnp.exp(alog_ref[...]))
    ts = da.shape[0]
    r = lax.broadcasted_iota(I32, (ts, ts), 0)
    c = lax.broadcasted_iota(I32, (ts, ts), 1)
    same_chunk = (r // SSD_CHUNK) == (c // SSD_CHUNK)
    tri = jnp.where(jnp.logical_and(r >= c, same_chunk), 1.0, 0.0).astype(BF16)
    hi, mid, lo = _split3(da)
    dt_ref[...] = dt
    cs_ref[...] = _dot(tri, hi) + _dot(tri, mid) + _dot(tri, lo)


def _ssd_dt(x, w_dt, dt_bias, a_log, ts):
    t, d = x.shape
    nh = w_dt.shape[1]
    w_pad = jnp.zeros((d, LANES), F32).at[:, :nh].set(w_dt)
    wh, wl = _split2(w_pad)
    bias = jnp.zeros((1, LANES), F32).at[0, :nh].set(dt_bias)
    alog = jnp.zeros((1, LANES), F32).at[0, :nh].set(a_log)
    fixed = lambda i: (0, 0)
    row = lambda i: (i, 0)
    return pl.pallas_call(
        _ssd_dt_kernel,
        grid=(t // ts,),
        in_specs=[pl.BlockSpec((ts, d), row), pl.BlockSpec((d, LANES), fixed),
                  pl.BlockSpec((d, LANES), fixed), pl.BlockSpec((1, LANES), fixed),
                  pl.BlockSpec((1, LANES), fixed)],
        out_specs=[pl.BlockSpec((ts, LANES), row), pl.BlockSpec((ts, LANES), row)],
        out_shape=[jax.ShapeDtypeStruct((t, LANES), F32), jax.ShapeDtypeStruct((t, LANES), F32)],
        compiler_params=_cp(("parallel",)),
        name="ssd_dt_prep",
    )(x, wh, wl, bias, alog)


def _ssd_scan_kernel(xs_ref, bm_ref, cm_ref, z_ref, dt_ref, cs_ref, cst_ref, d_ref, ng_ref,
                     o_ref, h_ref):
    @pl.when(pl.program_id(2) == 0)
    def _():
        h_ref[...] = jnp.zeros_like(h_ref)

    lb, gw = xs_ref.shape
    hpg = gw // HEAD_DIM
    cl = SSD_CHUNK
    lane_head = lax.broadcasted_iota(I32, (cl, gw), 1) // HEAD_DIM
    row_head = lax.broadcasted_iota(I32, (1, gw), 1) // HEAD_DIM
    r = lax.broadcasted_iota(I32, (cl, cl), 0)
    c = lax.broadcasted_iota(I32, (cl, cl), 1)
    causal = r >= c

    def expand(cols, heads):
        out = cols[:, hpg - 1:hpg]
        for hh in range(hpg - 2, -1, -1):
            out = jnp.where(heads == hh, cols[:, hh:hh + 1], out)
        return out

    for ci in range(lb // cl):
        sl = slice(ci * cl, (ci + 1) * cl)
        x = xs_ref[sl, :].astype(F32)
        bm = bm_ref[sl, :]
        cm = cm_ref[sl, :]
        dt = dt_ref[0, sl, :]
        cs = cs_ref[0, sl, :]
        cst = cst_ref[0, :, sl]
        cs_e = expand(cs, lane_head)
        end_e = expand(cs[cl - 1:cl, :], row_head)
        xdt = x * expand(dt, lane_head)
        cb = _dot_nt(cm, bm)
        ms = []
        for hh in range(hpg):
            seg = cs[:, hh:hh + 1] - cst[hh:hh + 1, :]
            decay = jnp.exp(jnp.where(causal, seg, -jnp.inf))
            ms.append((cb * decay).astype(BF16))
        y_all = _dot(jnp.concatenate(ms, axis=0), xdt.astype(BF16))
        y = y_all[(hpg - 1) * cl:hpg * cl, :]
        for hh in range(hpg - 2, -1, -1):
            y = jnp.where(lane_head == hh, y_all[hh * cl:(hh + 1) * cl, :], y)
        h_in = h_ref[...]
        y = y + _dot(cm, h_in.astype(BF16)) * jnp.exp(cs_e)
        y = y + d_ref[0] * x
        states = _dot_tn(bm, (xdt * jnp.exp(end_e - cs_e)).astype(BF16))
        h_ref[...] = jnp.exp(end_e) * h_in + states
        y = y * _silu(z_ref[sl, :].astype(F32))
        y = y * lax.rsqrt(jnp.mean(y * y, axis=-1, keepdims=True) + RMS_EPS)
        o_ref[sl, :] = (y * ng_ref[...]).astype(o_ref.dtype)


def _ssd_scan(xbc, z, dt, cs, d_skip, norm_g, bsz, seq, lb):
    t = xbc.shape[0]
    d_inner = z.shape[1]
    g = SSD_GROUPS
    gw = d_inner // g
    hpg = gw // HEAD_DIM
    n = SSD_STATE
    nsb = seq // lb
    dt_g = dt.reshape(t, g, hpg).transpose(1, 0, 2)
    cs_g = cs.reshape(t, g, hpg).transpose(1, 0, 2)
    cs_t = cs.reshape(t, g, hpg).transpose(1, 2, 0)
    d_e = jnp.repeat(d_skip.astype(F32), HEAD_DIM).reshape(g, 1, gw)
    rows = lambda b, gi, s: b * nsb + s
    return pl.pallas_call(
        _ssd_scan_kernel,
        grid=(bsz, g, nsb),
        in_specs=[pl.BlockSpec((lb, gw), lambda b, gi, s: (rows(b, gi, s), gi)),
                  pl.BlockSpec((lb, n), lambda b, gi, s: (rows(b, gi, s), d_inner // n + gi)),
                  pl.BlockSpec((lb, n), lambda b, gi, s: (rows(b, gi, s), d_inner // n + g + gi)),
                  pl.BlockSpec((lb, gw), lambda b, gi, s: (rows(b, gi, s), gi)),
                  pl.BlockSpec((1, lb, hpg), lambda b, gi, s: (gi, rows(b, gi, s), 0)),
                  pl.BlockSpec((1, lb, hpg), lambda b, gi, s: (gi, rows(b, gi, s), 0)),
                  pl.BlockSpec((1, hpg, lb), lambda b, gi, s: (gi, 0, rows(b, gi, s))),
                  pl.BlockSpec((1, 1, gw), lambda b, gi, s: (gi, 0, 0)),
                  pl.BlockSpec((1, gw), lambda b, gi, s: (0, gi))],
        out_specs=pl.BlockSpec((lb, gw), lambda b, gi, s: (rows(b, gi, s), gi)),
        out_shape=jax.ShapeDtypeStruct((t, d_inner), BF16),
        scratch_shapes=[pltpu.VMEM((n, gw), F32)],
        compiler_params=_cp(("parallel", "parallel", "arbitrary")),
        name="ssd_chunk_scan",
    )(xbc, xbc, xbc, z, dt_g, cs_g, cs_t, d_e, norm_g.reshape(1, d_inner))


def _route_kernel(x_ref, rh_ref, rl_ref, idx_ref, gw_ref, rank_ref, cnt_ref, carry_ref):
    @pl.when(pl.program_id(0) == 0)
    def _():
        carry_ref[...] = jnp.zeros_like(carry_ref)

    xh, xl = _split2(x_ref[...])
    rh = rh_ref[...]
    logits = _dot_nt(rh, xh) + _dot_nt(rh, xl) + _dot_nt(rl_ref[...], xh)
    ne, tm = logits.shape
    e = lax.broadcasted_iota(I32, (ne, tm), 0)
    m1 = jnp.max(logits, axis=0, keepdims=True)
    i1 = jnp.min(jnp.where(logits == m1, e, ne), axis=0, keepdims=True)
    rest = jnp.where(e == i1, -jnp.inf, logits)
    m2 = jnp.max(rest, axis=0, keepdims=True)
    i2 = jnp.min(jnp.where(rest == m2, e, ne), axis=0, keepdims=True)
    ex = jnp.exp(m2 - m1)
    w1 = 1.0 / (1.0 + ex)
    w2 = ex / (1.0 + ex)
    sel1 = e == i1
    sel2 = e == i2
    sel = jnp.where(jnp.logical_or(sel1, sel2), 1.0, 0.0)
    r = lax.broadcasted_iota(I32, (tm, tm), 0)
    c = lax.broadcasted_iota(I32, (tm, tm), 1)
    before = jnp.where(r < c, 1.0, 0.0).astype(BF16)
    base = carry_ref[...][:, 0:1]
    rank = _dot(sel.astype(BF16), before) + base
    r1 = jnp.sum(jnp.where(sel1, rank, 0.0), axis=0, keepdims=True)
    r2 = jnp.sum(jnp.where(sel2, rank, 0.0), axis=0, keepdims=True)
    idx_ref[...] = jnp.concatenate([i1, i2], axis=0)
    gw_ref[...] = jnp.concatenate([w1, w2], axis=0)
    rank_ref[...] = jnp.concatenate([r1, r2], axis=0).astype(I32)
    total = carry_ref[...] + jnp.sum(sel, axis=1, keepdims=True)
    carry_ref[...] = total
    cnt_ref[...] = total.astype(I32)


def _moe_route(x, router, tm):
    t, d = x.shape
    ne = router.shape[1]
    rh, rl = _split2(router.T)
    fixed = lambda i: (0, 0)
    col = lambda i: (0, i)
    return pl.pallas_call(
        _route_kernel,
        grid=(t // tm,),
        in_specs=[pl.BlockSpec((tm, d), lambda i: (i, 0)),
                  pl.BlockSpec((ne, d), fixed), pl.BlockSpec((ne, d), fixed)],
        out_specs=[pl.BlockSpec((TOP_K, tm), col), pl.BlockSpec((TOP_K, tm), col),
                   pl.BlockSpec((TOP_K, tm), col), pl.BlockSpec((ne, LANES), fixed)],
        out_shape=[jax.ShapeDtypeStruct((TOP_K, t), I32), jax.ShapeDtypeStruct((TOP_K, t), F32),
                   jax.ShapeDtypeStruct((TOP_K, t), I32), jax.ShapeDtypeStruct((ne, LANES), I32)],
        scratch_shapes=[pltpu.VMEM((ne, LANES), F32)],
        compiler_params=_cp(("arbitrary",)),
        name="moe_route_top2",
    )(x, rh, rl)


def _group_plan(idx, rank, counts, tm, n_tiles):
    ne = counts.shape[0]
    t = idx.shape[1]
    tiles_e = (counts + tm - 1) // tm
    tile_end = jnp.cumsum(tiles_e)
    tile_start = tile_end - tiles_e
    start_of = jnp.sum(jnp.where(idx[:, :, None] == jnp.arange(ne, dtype=I32), tile_start, 0), axis=-1)
    slots = (start_of * tm + rank).reshape(-1)
    tile_ids = jnp.arange(n_tiles + 1, dtype=I32)
    tile_expert = jnp.minimum(jnp.searchsorted(tile_end, tile_ids, side="right"), ne - 1).astype(I32)
    spare = TOP_K * t + jnp.arange(tm, dtype=I32)
    dst = jnp.tile(spare, n_tiles).at[slots].set(jnp.arange(TOP_K * t, dtype=I32))
    src = jnp.where(dst < TOP_K * t, dst % t, 0)
    src_tab = jnp.concatenate([src, jnp.zeros((tm,), I32)]).reshape(n_tiles + 1, 1, tm)
    dst_tab = jnp.concatenate([spare, dst]).reshape(n_tiles + 1, 1, tm)
    return tile_expert, src_tab, dst_tab


def _pick(pref, n):
    return pref if n % pref == 0 else n


def kernel(x, p, ln_mix_g, ln_mix_b, ln_ffn_g, ln_ffn_b, fox_w_in, fox_b_f, fox_w_o, ssd_w_in, ssd_conv_w, ssd_conv_b, ssd_dt_bias, ssd_a_log, ssd_d, ssd_norm_g, ssd_w_out, ffn_w_gate, ffn_w_up, ffn_w_down, moe_router, moe_w_gate, moe_w_up, moe_w_down, ple_w_proj, ple_w_gate):
    bsz, seq, d = x.shape
    depth = p.shape[0]
    t = bsz * seq
    alpha = (2.0 * depth) ** 0.25
    x2 = x.reshape(t, d)
    p2 = p.reshape(depth, t, p.shape[-1])
    tm_big = _pick(1024, t)
    tm = _pick(512, t)
    ts = _pick(512, seq)
    one_tile = jnp.zeros((t // tm,), I32)
    all_valid = jnp.ones((t // tm,), I32)

    w_in = fox_w_in[0]
    qkv = _qkv_proj(x2, w_in[:, :3 * d].astype(BF16), tm_big)
    cum = _fgate_cumlog(x2, w_in[:, 3 * d:], fox_b_f[0], bsz, seq, ts)
    attn = _fox_attention(qkv, cum, bsz, seq, ts)
    h = _proj_residual_ln(attn, fox_w_o[0].astype(BF16), x2, ln_mix_g[0], ln_mix_b[0], alpha, tm,
                          "fox_out_proj_ln")
    y = _swiglu_tiles(h, ffn_w_gate.astype(BF16), ffn_w_up.astype(BF16), ffn_w_down.astype(BF16),
                      one_tile, all_valid, tm, 256, "ffn_swiglu_dense")
    h = _post_dense(h, y, p2[0], ln_ffn_g[0], ln_ffn_b[0], ple_w_proj[0].astype(BF16),
                    ple_w_gate[0].astype(BF16), alpha, tm)

    w_in = ssd_w_in[0]
    d_inner = ssd_norm_g.shape[1]
    conv_ch = ssd_conv_w.shape[2]
    z = _ssd_inproj(h, w_in[:, :d_inner].astype(BF16), tm_big, _pick(1024, d_inner), "ssd_inproj_z")
    xbc = _ssd_inproj_conv(h, w_in[:, d_inner:d_inner + conv_ch].astype(BF16), ssd_conv_w[0],
                           ssd_conv_b[0], seq, _pick(1024, seq), _pick(1024, conv_ch))
    dt, cs = _ssd_dt(h, w_in[:, d_inner + conv_ch:], ssd_dt_bias[0], ssd_a_log[0], ts)
    nh = ssd_dt_bias.shape[1]
    ymix = _ssd_scan(xbc, z, dt[:, :nh], cs[:, :nh], ssd_d[0], ssd_norm_g[0], bsz, seq, ts)
    h = _proj_residual_ln(ymix, ssd_w_out[0].astype(BF16), h, ln_mix_g[1], ln_mix_b[1], alpha, tm,
                          "ssd_out_proj_ln")
    ne = moe_router.shape[2]
    idx, gate_w, rank, counts = _moe_route(h, moe_router[0], tm)
    n_tiles = (TOP_K * t) // tm + ne
    tile_expert, src_tab, dst_tab = _group_plan(idx, rank, counts[:, 0], tm, n_tiles)
    y_tok = _moe_experts(h, moe_w_gate[0].astype(BF16), moe_w_up[0].astype(BF16),
                         moe_w_down[0].astype(BF16), tile_expert, src_tab, dst_tab,
                         TOP_K * t + tm, tm, 512)
    h = _post_moe(h, y_tok, gate_w.T, p2[1], ln_ffn_g[1], ln_ffn_b[1],
                  ple_w_proj[1].astype(BF16), ple_w_gate[1].astype(BF16), alpha, tm)
    return h.reshape(bsz, seq, d)
```

```python
import functools

import jax
import jax.numpy as jnp
from jax import lax
from jax.experimental import pallas as pl
from jax.experimental.pallas import tpu as pltpu

F32 = jnp.float32
BF16 = jnp.bfloat16
I32 = jnp.int32

LN_EPS = 1e-5
RMS_EPS = 1e-5
LANES = 128
HEAD_DIM = 64
SSD_GROUPS = 8
SSD_STATE = 128
SSD_CONV = 4
SSD_CHUNK = 128
TOP_K = 2
LOG2E = 1.4426950408889634
VMEM_LIMIT = 56 * 1024 * 1024


def _cp(sem, vmem=VMEM_LIMIT):
    return pltpu.CompilerParams(dimension_semantics=sem, vmem_limit_bytes=vmem)


def _dot(a, b):
    return jnp.dot(a, b, preferred_element_type=F32)


def _dot_nt(a, b):
    return lax.dot_general(a, b, (((1,), (1,)), ((), ())), preferred_element_type=F32)


def _dot_tn(a, b):
    return lax.dot_general(a, b, (((0,), (0,)), ((), ())), preferred_element_type=F32)


def _split2(v):
    hi = v.astype(BF16)
    lo = (v - hi.astype(F32)).astype(BF16)
    return hi, lo


def _split3(v):
    hi = v.astype(BF16)
    r = v - hi.astype(F32)
    mid = r.astype(BF16)
    lo = (r - mid.astype(F32)).astype(BF16)
    return hi, mid, lo


def _dot_x2w2(x, wh, wl):
    xh, xl = _split2(x)
    return _dot(xh, wh) + _dot(xl, wh) + _dot(xh, wl)


def _layer_norm(v, g, b):
    mu = jnp.mean(v, axis=-1, keepdims=True)
    vc = v - mu
    var = jnp.mean(vc * vc, axis=-1, keepdims=True)
    return vc * lax.rsqrt(var + LN_EPS) * g + b


def _silu(v):
    h = 0.5 * v
    return h + h * jnp.tanh(h)


def _softplus(v):
    return jnp.maximum(v, 0.0) + jnp.log1p(jnp.exp(-jnp.abs(v)))


def _qkv_kernel(x_ref, w_ref, o_ref, *, scale):
    j = pl.program_id(1)
    acc = _dot(x_ref[...].astype(BF16), w_ref[...])
    acc = acc * jnp.where(j == 0, scale, 1.0).astype(F32)
    for c in range(o_ref.shape[0]):
        o_ref[c] = acc[:, c * LANES:(c + 1) * LANES].astype(BF16)


def _qkv_proj(x, w_qkv, tm):
    t, d = x.shape
    n = w_qkv.shape[1]
    tn = d
    cpb = tn // LANES
    return pl.pallas_call(
        functools.partial(_qkv_kernel, scale=LOG2E * HEAD_DIM ** -0.5),
        grid=(t // tm, n // tn),
        in_specs=[pl.BlockSpec((tm, d), lambda i, j: (i, 0)),
                  pl.BlockSpec((d, tn), lambda i, j: (0, j))],
        out_specs=pl.BlockSpec((cpb, tm, LANES), lambda i, j: (j, i, 0)),
        out_shape=jax.ShapeDtypeStruct((n // LANES, t, LANES), BF16),
        compiler_params=_cp(("parallel", "arbitrary")),
        name="fox_qkv_proj",
    )(x, w_qkv)


def _fgate_kernel(x_ref, wh_ref, wl_ref, b_ref, o_ref, carry_ref):
    @pl.when(pl.program_id(1) == 0)
    def _():
        carry_ref[...] = jnp.zeros_like(carry_ref)

    z = _dot_x2w2(x_ref[...], wh_ref[...], wl_ref[...]) + b_ref[...]
    log_f = jnp.minimum(z, 0.0) - jnp.log1p(jnp.exp(-jnp.abs(z)))
    ts = log_f.shape[0]
    r = lax.broadcasted_iota(I32, (ts, ts), 0)
    c = lax.broadcasted_iota(I32, (ts, ts), 1)
    tri = jnp.where(r >= c, 1.0, 0.0).astype(BF16)
    hi, mid, lo = _split3(log_f)
    cs = _dot(tri, hi) + _dot(tri, mid) + _dot(tri, lo) + carry_ref[...]
    o_ref[...] = cs
    carry_ref[...] = cs[ts - 1:ts, :]


def _fgate_cumlog(x, w_f, b_f, bsz, seq, ts):
    t, d = x.shape
    nh = w_f.shape[1]
    w_pad = jnp.zeros((d, LANES), F32).at[:, :nh].set(w_f)
    wh, wl = _split2(w_pad)
    b_pad = jnp.zeros((1, LANES), F32).at[0, :nh].set(b_f)
    nsb = seq // ts
    return pl.pallas_call(
        _fgate_kernel,
        grid=(bsz, nsb),
        in_specs=[pl.BlockSpec((ts, d), lambda b, s: (b * nsb + s, 0)),
                  pl.BlockSpec((d, LANES), lambda b, s: (0, 0)),
                  pl.BlockSpec((d, LANES), lambda b, s: (0, 0)),
                  pl.BlockSpec((1, LANES), lambda b, s: (0, 0))],
        out_specs=pl.BlockSpec((ts, LANES), lambda b, s: (b * nsb + s, 0)),
        out_shape=jax.ShapeDtypeStruct((t, LANES), F32),
        scratch_shapes=[pltpu.VMEM((1, LANES), F32)],
        compiler_params=_cp(("parallel", "arbitrary")),
        name="fox_forget_cumlog",
    )(x, wh, wl, b_pad)


def _pieces3(v):
    hi = v.astype(BF16).astype(F32)
    r = v - hi
    mid = r.astype(BF16).astype(F32)
    lo = (r - mid).astype(BF16).astype(F32)
    return hi, mid, lo


AUG_ROWS = 16


def _aug_rows(rows):
    n = rows[0].shape[1]
    pad = [jnp.zeros((AUG_ROWS - len(rows), n), F32)]
    return jnp.concatenate(list(rows) + pad, axis=0).astype(BF16)


def _row_to_lane(base):
    r = lax.broadcasted_iota(I32, (AUG_ROWS, LANES), 0)
    lane = lax.broadcasted_iota(I32, (AUG_ROWS, LANES), 1)
    return jnp.where(lane == base + r, 1.0, 0.0).astype(BF16)


def _fox_kernel(q_ref, k_ref, v_ref, c_ref, o_ref, qa_ref, ka_ref, vt_ref, sa_ref, sb_ref, p_ref,
                m_ref, alpha_ref, acc_ref, *, tq, cb):
    qi = pl.program_id(2)
    seq = k_ref.shape[1]
    bases = (HEAD_DIM, 0)

    def data_lanes(lane, h):
        return (lane < HEAD_DIM) if h == 0 else (lane >= HEAD_DIM)

    lane = lax.broadcasted_iota(I32, (tq, LANES), 1)
    ones = jnp.ones((1, tq), F32)
    r128 = lax.broadcasted_iota(I32, (LANES, LANES), 0)
    c128 = lax.broadcasted_iota(I32, (LANES, LANES), 1)
    eye = jnp.where(r128 == c128, 1.0, 0.0).astype(BF16)

    @pl.when(qi == 0)
    def _():
        for c0 in range(0, seq, tq):
            rows = slice(c0, c0 + tq)
            k = k_ref[0, rows, :]
            v = v_ref[0, rows, :]
            hi, mid, lo = _pieces3(c_ref[0, 0, :, rows] * (-LOG2E))
            for h in range(2):
                k_rows = _aug_rows([ones, ones, ones, hi[h:h + 1], mid[h:h + 1], lo[h:h + 1]])
                k_aug = _dot_tn(k_rows, _row_to_lane(bases[h]))
                v_aug = jnp.where(lane == bases[h], 1.0, 0.0).astype(BF16)
                ka_ref[h, rows, :] = jnp.where(data_lanes(lane, h), k, k_aug.astype(BF16))
                vt_ref[h, :, rows] = _dot_nt(eye, jnp.where(data_lanes(lane, h), v, v_aug)).astype(BF16)

    q = q_ref[0]
    hi, mid, lo = _pieces3(c_ref[0, 0, :, pl.ds(pl.multiple_of(qi * tq, tq), tq)] * LOG2E)
    for h in range(2):
        q_rows = _aug_rows([hi[h:h + 1], mid[h:h + 1], lo[h:h + 1], ones, ones, ones])
        q_aug = _dot_tn(q_rows, _row_to_lane(bases[h]))
        qa_ref[h] = jnp.where(data_lanes(lane, h), q, q_aug.astype(BF16))
    m_ref[...] = jnp.full(m_ref.shape, -jnp.inf, F32)
    acc_ref[...] = jnp.zeros(acc_ref.shape, F32)

    def scores(j, s_ref):
        off = pl.multiple_of(j * tq, tq)
        for h in range(2):
            s_ref[h] = _dot_nt(ka_ref[h, pl.ds(off, tq), :], qa_ref[h])

    def softmax_pv(j, s_ref, diagonal):
        off = pl.multiple_of(j * tq, tq)
        for h in range(2):
            for c0 in range(0, tq, cb):
                cols = slice(c0, c0 + cb)
                nrow = c0 + cb if diagonal else tq
                s = s_ref[h, 0:nrow, cols]
                if diagonal:
                    key = lax.broadcasted_iota(I32, s.shape, 0)
                    qry = c0 + lax.broadcasted_iota(I32, s.shape, 1)
                    s = jnp.where(key <= qry, s, -jnp.inf)
                m_old = m_ref[h, :, cols]
                m_new = jnp.maximum(m_old, jnp.max(s, axis=0, keepdims=True))
                m_ref[h, :, cols] = m_new
                alpha_ref[h, :, cols] = jnp.exp2(m_old - m_new)
                p_ref[h, 0:nrow, cols] = jnp.exp2(s - m_new).astype(BF16)
                if nrow < tq:
                    p_ref[h, nrow:tq, cols] = jnp.zeros((tq - nrow, cb), BF16)
            acc_ref[h] = alpha_ref[h] * acc_ref[h] + _dot(vt_ref[h, :, pl.ds(off, tq)], p_ref[h])

    scores(0, sa_ref)

    def pair(i, carry):
        scores(2 * i + 1, sb_ref)
        softmax_pv(2 * i, sa_ref, False)
        scores(2 * i + 2, sa_ref)
        softmax_pv(2 * i + 1, sb_ref, False)
        return carry

    lax.fori_loop(0, qi // 2, pair, 0)

    @pl.when(qi % 2 == 0)
    def _():
        softmax_pv(qi, sa_ref, True)

    @pl.when(qi % 2 == 1)
    def _():
        scores(qi, sb_ref)
        softmax_pv(qi - 1, sa_ref, False)
        softmax_pv(qi, sb_ref, True)

    a0 = acc_ref[0]
    a1 = acc_ref[1]
    row = lax.broadcasted_iota(I32, a0.shape, 0)
    out_t = jnp.where(row < HEAD_DIM, a0 / a0[bases[0]:bases[0] + 1, :],
                      a1 / a1[bases[1]:bases[1] + 1, :])
    o_ref[...] = _dot_tn(out_t.astype(BF16), eye).astype(o_ref.dtype)


def _fox_attention(qkv, cum, bsz, seq, tq):
    hp = qkv.shape[0] // 3
    t = qkv.shape[1]
    c4 = cum[:, :2 * hp].reshape(bsz, seq, hp, 2).transpose(0, 2, 3, 1)
    nq = seq // tq
    return pl.pallas_call(
        functools.partial(_fox_kernel, tq=tq, cb=LANES),
        grid=(bsz, hp, nq),
        in_specs=[pl.BlockSpec((1, tq, LANES), lambda b, h, i: (h, b * nq + i, 0)),
                  pl.BlockSpec((1, seq, LANES), lambda b, h, i: (hp + h, b, 0)),
                  pl.BlockSpec((1, seq, LANES), lambda b, h, i: (2 * hp + h, b, 0)),
                  pl.BlockSpec((1, 1, 2, seq), lambda b, h, i: (b, h, 0, 0))],
        out_specs=pl.BlockSpec((tq, LANES), lambda b, h, i: (b * nq + i, h)),
        out_shape=jax.ShapeDtypeStruct((t, hp * LANES), BF16),
        scratch_shapes=[pltpu.VMEM((2, tq, LANES), BF16),
                        pltpu.VMEM((2, seq, LANES), BF16),
                        pltpu.VMEM((2, LANES, seq), BF16),
                        pltpu.VMEM((2, tq, tq), F32),
                        pltpu.VMEM((2, tq, tq), F32),
                        pltpu.VMEM((2, tq, tq), BF16),
                        pltpu.VMEM((2, 1, tq), F32),
                        pltpu.VMEM((2, 1, tq), F32),
                        pltpu.VMEM((2, LANES, tq), F32)],
        compiler_params=_cp(("parallel", "parallel", "arbitrary")),
        name="fox_flash_attention",
    )(qkv, qkv, qkv, c4)


def _proj_ln_kernel(a_ref, w_ref, res_ref, g_ref, b_ref, o_ref, *, alpha):
    y = _dot(a_ref[...], w_ref[...])
    o_ref[...] = _layer_norm(alpha * res_ref[...] + y, g_ref[...], b_ref[...])


def _proj_residual_ln(a, w, res, g, b, alpha, tm, name):
    t, k = a.shape
    d = w.shape[1]
    return pl.pallas_call(
        functools.partial(_proj_ln_kernel, alpha=alpha),
        grid=(t // tm,),
        in_specs=[pl.BlockSpec((tm, k), lambda i: (i, 0)),
                  pl.BlockSpec((k, d), lambda i: (0, 0)),
                  pl.BlockSpec((tm, d), lambda i: (i, 0)),
                  pl.BlockSpec((1, d), lambda i: (0, 0)),
                  pl.BlockSpec((1, d), lambda i: (0, 0))],
        out_specs=pl.BlockSpec((tm, d), lambda i: (i, 0)),
        out_shape=jax.ShapeDtypeStruct((t, d), F32),
        compiler_params=_cp(("parallel",)),
        name=name,
    )(a, w, res, g.reshape(1, d), b.reshape(1, d))


def _swiglu_kernel(te_ref, tv_ref, x_ref, wg_ref, wu_ref, wd_ref, o_ref, *, fchunk):
    i = pl.program_id(0)

    @pl.when(tv_ref[i] == 0)
    def _():
        o_ref[...] = jnp.zeros_like(o_ref)

    @pl.when(tv_ref[i] != 0)
    def _():
        x = x_ref[...].astype(BF16)
        f = wg_ref.shape[-1]
        for n, c0 in enumerate(range(0, f, fchunk)):
            c1 = min(c0 + fchunk, f)
            g = _dot(x, wg_ref[0, :, c0:c1])
            u = _dot(x, wu_ref[0, :, c0:c1])
            y = _dot((_silu(g) * u).astype(BF16), wd_ref[0, c0:c1, :])
            if n == 0:
                o_ref[...] = y
            else:
                o_ref[...] += y


def _swiglu_tiles(x, w_gate, w_up, w_down, tile_expert, tile_valid, tm, fchunk, name):
    r, d = x.shape
    f = w_gate.shape[-1]
    once = pl.Buffered(1)
    grid_spec = pltpu.PrefetchScalarGridSpec(
        num_scalar_prefetch=2,
        grid=(r // tm,),
        in_specs=[pl.BlockSpec((tm, d), lambda i, te, tv: (i, 0)),
                  pl.BlockSpec((1, d, f), lambda i, te, tv: (te[i], 0, 0), pipeline_mode=once),
                  pl.BlockSpec((1, d, f), lambda i, te, tv: (te[i], 0, 0), pipeline_mode=once),
                  pl.BlockSpec((1, f, d), lambda i, te, tv: (te[i], 0, 0), pipeline_mode=once)],
        out_specs=pl.BlockSpec((tm, d), lambda i, te, tv: (i, 0)),
    )
    return pl.pallas_call(
        functools.partial(_swiglu_kernel, fchunk=fchunk),
        grid_spec=grid_spec,
        out_shape=jax.ShapeDtypeStruct((r, d), F32),
        compiler_params=_cp(("arbitrary",)),
        name=name,
    )(tile_expert, tile_valid, x, w_gate, w_up, w_down)


def _ple(h, p_ref, wp_ref, wgt_ref):
    gate = jax.nn.sigmoid(_dot(h.astype(BF16), wgt_ref[...]))
    proj = _dot(p_ref[...].astype(BF16), wp_ref[...])
    return h + proj * gate


def _post_dense_kernel(res_ref, y_ref, p_ref, g_ref, b_ref, wp_ref, wgt_ref, o_ref, *, alpha):
    h = _layer_norm(alpha * res_ref[...] + y_ref[...], g_ref[...], b_ref[...])
    o_ref[...] = _ple(h, p_ref, wp_ref, wgt_ref)


def _post_dense(res, y, p, g, b, w_proj, w_gate, alpha, tm):
    t, d = res.shape
    pd = p.shape[1]
    row = lambda i: (i, 0)
    fixed = lambda i: (0, 0)
    return pl.pallas_call(
        functools.partial(_post_dense_kernel, alpha=alpha),
        grid=(t // tm,),
        in_specs=[pl.BlockSpec((tm, d), row), pl.BlockSpec((tm, d), row),
                  pl.BlockSpec((tm, pd), row),
                  pl.BlockSpec((1, d), fixed), pl.BlockSpec((1, d), fixed),
                  pl.BlockSpec((pd, d), fixed), pl.BlockSpec((d, d), fixed)],
        out_specs=pl.BlockSpec((tm, d), row),
        out_shape=jax.ShapeDtypeStruct((t, d), F32),
        compiler_params=_cp(("parallel",)),
        name="post_ffn_dense",
    )(res, y, p, g.reshape(1, d), b.reshape(1, d), w_proj, w_gate)


def _row_copy(src_hbm, row, dst, r, sem):
    return pltpu.make_async_copy(src_hbm.at[pl.ds(row, 1), :], dst.at[pl.ds(r, 1), :], sem)


def _post_moe_kernel(res_ref, y1_ref, y2_ref, gw_ref, p_ref, g_ref, b_ref, wp_ref, wgt_ref,
                     o_ref, *, alpha):
    gw = gw_ref[...]
    ffn = gw[:, 0:1] * y1_ref[...] + gw[:, 1:2] * y2_ref[...]
    h = _layer_norm(alpha * res_ref[...] + ffn, g_ref[...], b_ref[...])
    o_ref[...] = _ple(h, p_ref, wp_ref, wgt_ref)


def _post_moe(res, y_tok, gate_w, p, g, b, w_proj, w_gate, alpha, tm):
    t, d = res.shape
    pd = p.shape[1]
    nt = t // tm
    row = lambda i: (i, 0)
    fixed = lambda i: (0, 0)
    return pl.pallas_call(
        functools.partial(_post_moe_kernel, alpha=alpha),
        grid=(nt,),
        in_specs=[pl.BlockSpec((tm, d), row),
                  pl.BlockSpec((tm, d), row),
                  pl.BlockSpec((tm, d), lambda i: (nt + i, 0)),
                  pl.BlockSpec((tm, TOP_K), row),
                  pl.BlockSpec((tm, pd), row),
                  pl.BlockSpec((1, d), fixed), pl.BlockSpec((1, d), fixed),
                  pl.BlockSpec((pd, d), fixed), pl.BlockSpec((d, d), fixed)],
        out_specs=pl.BlockSpec((tm, d), row),
        out_shape=jax.ShapeDtypeStruct((t, d), F32),
        compiler_params=_cp(("parallel",)),
        name="post_ffn_moe_combine",
    )(res, y_tok, y_tok, gate_w, p, g.reshape(1, d), b.reshape(1, d), w_proj, w_gate)


def _moe_experts_kernel(te_ref, src_ref, nxt_ref, dst_ref, x_hbm, wg_ref, wu_ref, wd_ref, y_hbm,
                        xbuf0, xbuf1, ybuf0, ybuf1, xb_ref, gsem, ssem, *, fchunk):
    i = pl.program_id(0)
    last = te_ref[pl.num_programs(0)]
    tm = xbuf0.shape[0]
    xbufs = (xbuf0, xbuf1)
    ybufs = (ybuf0, ybuf1)

    def gather(row, buf, r, s):
        return _row_copy(x_hbm, row, buf, r, gsem.at[s])

    def scatter(buf, r, row, s):
        return pltpu.make_async_copy(buf.at[pl.ds(r, 1), :], y_hbm.at[pl.ds(row, 1), :], ssem.at[s])

    def wait_all(make):
        def body(r, _):
            make(r).wait()
            return 0
        lax.fori_loop(0, tm, body, 0, unroll=8)

    @pl.when(i == 0)
    def _():
        def body(r, _):
            gather(src_ref[0, 0, r], xbuf0, r, 0).start()
            return 0
        lax.fori_loop(0, tm, body, 0, unroll=8)
        ybuf1[...] = jnp.zeros_like(ybuf1)

    def step(s):
        o = 1 - s
        wait_all(lambda r: gather(0, xbufs[s], r, s))

        @pl.when(i >= 1)
        def _():
            wait_all(lambda r: scatter(ybufs[s], r, 0, s))

        for r in range(tm):
            gather(nxt_ref[0, 0, r], xbufs[o], r, o).start()
        for r in range(tm):
            scatter(ybufs[o], r, dst_ref[0, 0, r], o).start()
        xb_ref[...] = xbufs[s][...].astype(BF16)
        f = wg_ref.shape[-1]
        for n, c0 in enumerate(range(0, f, fchunk)):
            c1 = min(c0 + fchunk, f)
            g = _dot(xb_ref[...], wg_ref[0, :, c0:c1])
            u = _dot(xb_ref[...], wu_ref[0, :, c0:c1])
            y = _dot((_silu(g) * u).astype(BF16), wd_ref[0, c0:c1, :])
            if n == 0:
                ybufs[s][...] = y
            else:
                ybufs[s][...] += y

        @pl.when(i == last)
        def _():
            wait_all(lambda r: gather(0, xbufs[o], r, o))
            wait_all(lambda r: scatter(ybufs[o], r, 0, o))

    for s in range(2):
        pl.when(jnp.logical_and(i % 2 == s, i <= last))(functools.partial(step, s))


def _moe_experts(x, w_gate, w_up, w_down, tile_expert, src_tab, dst_tab, n_out_rows, tm, fchunk):
    d = x.shape[1]
    f = w_gate.shape[-1]
    steps = tile_expert.shape[0] - 1
    once = pl.Buffered(1)
    smem_tile = lambda index_map: pl.BlockSpec((1, 1, tm), index_map, memory_space=pltpu.SMEM)
    grid_spec = pltpu.PrefetchScalarGridSpec(
        num_scalar_prefetch=1,
        grid=(steps,),
        in_specs=[smem_tile(lambda i, te: (i, 0, 0)),
                  smem_tile(lambda i, te: (jnp.minimum(i + 1, steps - 1), 0, 0)),
                  smem_tile(lambda i, te: (i, 0, 0)),
                  pl.BlockSpec(memory_space=pl.ANY),
                  pl.BlockSpec((1, d, f), lambda i, te: (te[i], 0, 0), pipeline_mode=once),
                  pl.BlockSpec((1, d, f), lambda i, te: (te[i], 0, 0), pipeline_mode=once),
                  pl.BlockSpec((1, f, d), lambda i, te: (te[i], 0, 0), pipeline_mode=once)],
        out_specs=pl.BlockSpec(memory_space=pl.ANY),
        scratch_shapes=[pltpu.VMEM((tm, d), F32), pltpu.VMEM((tm, d), F32),
                        pltpu.VMEM((tm, d), F32), pltpu.VMEM((tm, d), F32),
                        pltpu.VMEM((tm, d), BF16),
                        pltpu.SemaphoreType.DMA((2,)), pltpu.SemaphoreType.DMA((2,))],
    )
    return pl.pallas_call(
        functools.partial(_moe_experts_kernel, fchunk=fchunk),
        grid_spec=grid_spec,
        out_shape=jax.ShapeDtypeStruct((n_out_rows, d), F32),
        compiler_params=_cp(("arbitrary",)),
        name="moe_experts_fused_dispatch",
    )(tile_expert, src_tab, src_tab, dst_tab, x, w_gate, w_up, w_down)


def _inproj_kernel(x_ref, w_ref, o_ref):
    o_ref[...] = _dot(x_ref[...].astype(BF16), w_ref[...]).astype(o_ref.dtype)


def _inproj_conv_kernel(x_ref, w_ref, cw_ref, cb_ref, o_ref, ext_ref, carry_ref, *, tiles_per_seq):
    i = pl.program_id(0)
    j = pl.program_id(1)
    tm = x_ref.shape[0]
    halo = carry_ref.shape[1]

    @pl.when(i % tiles_per_seq == 0)
    def _():
        carry_ref[j] = jnp.zeros(carry_ref.shape[1:], F32)

    acc = _dot(x_ref[...].astype(BF16), w_ref[...])
    ext_ref[0:halo, :] = carry_ref[j]
    ext_ref[halo:halo + tm, :] = acc
    cw = cw_ref[...]
    out = cb_ref[...] + cw[0:1, :] * ext_ref[halo - 3:halo - 3 + tm, :]
    for k in range(1, SSD_CONV):
        out = out + cw[k:k + 1, :] * ext_ref[halo - 3 + k:halo - 3 + k + tm, :]
    carry_ref[j] = acc[tm - halo:tm, :]
    o_ref[...] = _silu(out).astype(o_ref.dtype)


def _ssd_inproj(x, w, tm, tn, name):
    t, d = x.shape
    n = w.shape[1]
    return pl.pallas_call(
        _inproj_kernel,
        grid=(t // tm, n // tn),
        in_specs=[pl.BlockSpec((tm, d), lambda i, j: (i, 0)),
                  pl.BlockSpec((d, tn), lambda i, j: (0, j))],
        out_specs=pl.BlockSpec((tm, tn), lambda i, j: (i, j)),
        out_shape=jax.ShapeDtypeStruct((t, n), BF16),
        compiler_params=_cp(("parallel", "arbitrary")),
        name=name,
    )(x, w)


def _ssd_inproj_conv(x, w, conv_w, conv_b, seq, tm, tn):
    t, d = x.shape
    n = w.shape[1]
    halo = 8
    return pl.pallas_call(
        functools.partial(_inproj_conv_kernel, tiles_per_seq=seq // tm),
        grid=(t // tm, n // tn),
        in_specs=[pl.BlockSpec((tm, d), lambda i, j: (i, 0)),
                  pl.BlockSpec((d, tn), lambda i, j: (0, j)),
                  pl.BlockSpec((SSD_CONV, tn), lambda i, j: (0, j)),
                  pl.BlockSpec((1, tn), lambda i, j: (0, j))],
        out_specs=pl.BlockSpec((tm, tn), lambda i, j: (i, j)),
        out_shape=jax.ShapeDtypeStruct((t, n), BF16),
        scratch_shapes=[pltpu.VMEM((tm + halo, tn), F32), pltpu.VMEM((n // tn, halo, tn), F32)],
        compiler_params=_cp(("arbitrary", "arbitrary")),
        name="ssd_inproj_conv_silu",
    )(x, w, conv_w, conv_b.reshape(1, n))


def _ssd_dt_kernel(x_ref, wh_ref, wl_ref, bias_ref, alog_ref, dt_ref, cs_ref):
    dt = _softplus(_dot_x2w2(x_ref[...], wh_ref[...], wl_ref[...]) + bias_ref[...])
    da = dt * (-jnp.exp(alog_ref[...]))
    ts = da.shape[0]
    r = lax.broadcasted_iota(I32, (ts, ts), 0)
    c = lax.broadcasted_iota(I32, (ts, ts), 1)
    same_chunk = (r // SSD_CHUNK) == (c // SSD_CHUNK)
    tri = jnp.where(jnp.logical_and(r >= c, same_chunk), 1.0, 0.0).astype(BF16)
    hi, mid, lo = _split3(da)
    dt_ref[...] = dt
    cs_ref[...] = _dot(tri, hi) + _dot(tri, mid) + _dot(tri, lo)


def _ssd_dt(x, w_dt, dt_bias, a_log, ts):
    t, d = x.shape
    nh = w_dt.shape[1]
    w_pad = jnp.zeros((d, LANES), F32).at[:, :nh].set(w_dt)
    wh, wl = _split2(w_pad)
    bias = jnp.zeros((1, LANES), F32).at[0, :nh].set(dt_bias)
    alog = jnp.zeros((1, LANES), F32).at[0, :nh].set(a_log)
    fixed = lambda i: (0, 0)
    row = lambda i: (i, 0)
    return pl.pallas_call(
        _ssd_dt_kernel,
        grid=(t // ts,),
        in_specs=[pl.BlockSpec((ts, d), row), pl.BlockSpec((d, LANES), fixed),
                  pl.BlockSpec((d, LANES), fixed), pl.BlockSpec((1, LANES), fixed),
                  pl.BlockSpec((1, LANES), fixed)],
        out_specs=[pl.BlockSpec((ts, LANES), row), pl.BlockSpec((ts, LANES), row)],
        out_shape=[jax.ShapeDtypeStruct((t, LANES), F32), jax.ShapeDtypeStruct((t, LANES), F32)],
        compiler_params=_cp(("parallel",)),
        name="ssd_dt_prep",
    )(x, wh, wl, bias, alog)


def _ssd_scan_kernel(xs_ref, bm_ref, cm_ref, z_ref, dt_ref, cs_ref, cst_ref, d_ref, ng_ref,
                     o_ref, h_ref):
    @pl.when(pl.program_id(2) == 0)
    def _():
        h_ref[...] = jnp.zeros_like(h_ref)

    lb, gw = xs_ref.shape
    hpg = gw // HEAD_DIM
    cl = SSD_CHUNK
    lane_head = lax.broadcasted_iota(I32, (cl, gw), 1) // HEAD_DIM
    row_head = lax.broadcasted_iota(I32, (1, gw), 1) // HEAD_DIM
    r = lax.broadcasted_iota(I32, (cl, cl), 0)
    c = lax.broadcasted_iota(I32, (cl, cl), 1)
    causal = r >= c

    def expand(cols, heads):
        out = cols[:, hpg - 1:hpg]
        for hh in range(hpg - 2, -1, -1):
            out = jnp.where(heads == hh, cols[:, hh:hh + 1], out)
        return out

    for ci in range(lb // cl):
        sl = slice(ci * cl, (ci + 1) * cl)
        x = xs_ref[sl, :].astype(F32)
        bm = bm_ref[sl, :]
        cm = cm_ref[sl, :]
        dt = dt_ref[0, sl, :]
        cs = cs_ref[0, sl, :]
        cst = cst_ref[0, :, sl]
        cs_e = expand(cs, lane_head)
        end_e = expand(cs[cl - 1:cl, :], row_head)
        xdt = x * expand(dt, lane_head)
        cb = _dot_nt(cm, bm)
        ms = []
        for hh in range(hpg):
            seg = cs[:, hh:hh + 1] - cst[hh:hh + 1, :]
            decay = jnp.exp(jnp.where(causal, seg, -jnp.inf))
            ms.append((cb * decay).astype(BF16))
        y_all = _dot(jnp.concatenate(ms, axis=0), xdt.astype(BF16))
        y = y_all[(hpg - 1) * cl:hpg * cl, :]
        for hh in range(hpg - 2, -1, -1):
            y = jnp.where(lane_head == hh, y_all[hh * cl:(hh + 1) * cl, :], y)
        h_in = h_ref[...]
        y = y + _dot(cm, h_in.astype(BF16)) * jnp.exp(cs_e)
        y = y + d_ref[0] * x
        states = _dot_tn(bm, (xdt * jnp.exp(end_e - cs_e)).astype(BF16))
        h_ref[...] = jnp.exp(end_e) * h_in + states
        y = y * _silu(z_ref[sl, :].astype(F32))
        y = y * lax.rsqrt(jnp.mean(y * y, axis=-1, keepdims=True) + RMS_EPS)
        o_ref[sl, :] = (y * ng_ref[...]).astype(o_ref.dtype)


def _ssd_scan(xbc, z, dt, cs, d_skip, norm_g, bsz, seq, lb):
    t = xbc.shape[0]
    d_inner = z.shape[1]
    g = SSD_GROUPS
    gw = d_inner // g
    hpg = gw // HEAD_DIM
    n = SSD_STATE
    nsb = seq // lb
    dt_g = dt.reshape(t, g, hpg).transpose(1, 0, 2)
    cs_g = cs.reshape(t, g, hpg).transpose(1, 0, 2)
    cs_t = cs.reshape(t, g, hpg).transpose(1, 2, 0)
    d_e = jnp.repeat(d_skip.astype(F32), HEAD_DIM).reshape(g, 1, gw)
    rows = lambda b, gi, s: b * nsb + s
    return pl.pallas_call(
        _ssd_scan_kernel,
        grid=(bsz, g, nsb),
        in_specs=[pl.BlockSpec((lb, gw), lambda b, gi, s: (rows(b, gi, s), gi)),
                  pl.BlockSpec((lb, n), lambda b, gi, s: (rows(b, gi, s), d_inner // n + gi)),
                  pl.BlockSpec((lb, n), lambda b, gi, s: (rows(b, gi, s), d_inner // n + g + gi)),
                  pl.BlockSpec((lb, gw), lambda b, gi, s: (rows(b, gi, s), gi)),
                  pl.BlockSpec((1, lb, hpg), lambda b, gi, s: (gi, rows(b, gi, s), 0)),
                  pl.BlockSpec((1, lb, hpg), lambda b, gi, s: (gi, rows(b, gi, s), 0)),
                  pl.BlockSpec((1, hpg, lb), lambda b, gi, s: (gi, 0, rows(b, gi, s))),
                  pl.BlockSpec((1, 1, gw), lambda b, gi, s: (gi, 0, 0)),
                  pl.BlockSpec((1, gw), lambda b, gi, s: (0, gi))],
        out_specs=pl.BlockSpec((lb, gw), lambda b, gi, s: (rows(b, gi, s), gi)),
        out_shape=jax.ShapeDtypeStruct((t, d_inner), BF16),
        scratch_shapes=[pltpu.VMEM((n, gw), F32)],
        compiler_params=_cp(("parallel", "parallel", "arbitrary")),
        name="ssd_chunk_scan",
    )(xbc, xbc, xbc, z, dt_g, cs_g, cs_t, d_e, norm_g.reshape(1, d_inner))


def _route_kernel(x_ref, rh_ref, rl_ref, idx_ref, gw_ref, rank_ref, cnt_ref, carry_ref):
    @pl.when(pl.program_id(0) == 0)
    def _():
        carry_ref[...] = jnp.zeros_like(carry_ref)

    xh, xl = _split2(x_ref[...])
    rh = rh_ref[...]
    logits = _dot_nt(rh, xh) + _dot_nt(rh, xl) + _dot_nt(rl_ref[...], xh)
    ne, tm = logits.shape
    e = lax.broadcasted_iota(I32, (ne, tm), 0)
    m1 = jnp.max(logits, axis=0, keepdims=True)
    i1 = jnp.min(jnp.where(logits == m1, e, ne), axis=0, keepdims=True)
    rest = jnp.where(e == i1, -jnp.inf, logits)
    m2 = jnp.max(rest, axis=0, keepdims=True)
    i2 = jnp.min(jnp.where(rest == m2, e, ne), axis=0, keepdims=True)
    ex = jnp.exp(m2 - m1)
    w1 = 1.0 / (1.0 + ex)
    w2 = ex / (1.0 + ex)
    sel1 = e == i1
    sel2 = e == i2
    sel = jnp.where(jnp.logical_or(sel1, sel2), 1.0, 0.0)
    r = lax.broadcasted_iota(I32, (tm, tm), 0)
    c = lax.broadcasted_iota(I32, (tm, tm), 1)
    before = jnp.where(r < c, 1.0, 0.0).astype(BF16)
    base = carry_ref[...][:, 0:1]
    rank = _dot(sel.astype(BF16), before) + base
    r1 = jnp.sum(jnp.where(sel1, rank, 0.0), axis=0, keepdims=True)
    r2 = jnp.sum(jnp.where(sel2, rank, 0.0), axis=0, keepdims=True)
    idx_ref[...] = jnp.concatenate([i1, i2], axis=0)
    gw_ref[...] = jnp.concatenate([w1, w2], axis=0)
    rank_ref[...] = jnp.concatenate([r1, r2], axis=0).astype(I32)
    total = carry_ref[...] + jnp.sum(sel, axis=1, keepdims=True)
    carry_ref[...] = total
    cnt_ref[...] = total.astype(I32)


def _moe_route(x, router, tm):
    t, d = x.shape
    ne = router.shape[1]
    rh, rl = _split2(router.T)
    fixed = lambda i: (0, 0)
    col = lambda i: (0, i)
    return pl.pallas_call(
        _route_kernel,
        grid=(t // tm,),
        in_specs=[pl.BlockSpec((tm, d), lambda i: (i, 0)),
                  pl.BlockSpec((ne, d), fixed), pl.BlockSpec((ne, d), fixed)],
        out_specs=[pl.BlockSpec((TOP_K, tm), col), pl.BlockSpec((TOP_K, tm), col),
                   pl.BlockSpec((TOP_K, tm), col), pl.BlockSpec((ne, LANES), fixed)],
        out_shape=[jax.ShapeDtypeStruct((TOP_K, t), I32), jax.ShapeDtypeStruct((TOP_K, t), F32),
                   jax.ShapeDtypeStruct((TOP_K, t), I32), jax.ShapeDtypeStruct((ne, LANES), I32)],
        scratch_shapes=[pltpu.VMEM((ne, LANES), F32)],
        compiler_params=_cp(("arbitrary",)),
        name="moe_route_top2",
    )(x, rh, rl)


def _group_plan(idx, rank, counts, tm, n_tiles):
    ne = counts.shape[0]
    t = idx.shape[1]
    tiles_e = (counts + tm - 1) // tm
    tile_end = jnp.cumsum(tiles_e)
    tile_start = tile_end - tiles_e
    start_of = jnp.sum(jnp.where(idx[:, :, None] == jnp.arange(ne, dtype=I32), tile_start, 0), axis=-1)
    slots = (start_of * tm + rank).reshape(-1)
    tile_ids = jnp.arange(n_tiles + 1, dtype=I32)
    tile_expert = jnp.minimum(jnp.searchsorted(tile_end, tile_ids, side="right"), ne - 1).astype(I32)
    tile_expert = jnp.concatenate([tile_expert, tile_end[ne - 1:].astype(I32)])
    spare = TOP_K * t + jnp.arange(tm, dtype=I32)
    dst = jnp.tile(spare, n_tiles).at[slots].set(jnp.arange(TOP_K * t, dtype=I32),
                                                 unique_indices=True, mode="promise_in_bounds")
    src = jnp.where(dst < TOP_K * t, dst % t, 0)
    src_tab = jnp.concatenate([src, jnp.zeros((tm,), I32)]).reshape(n_tiles + 1, 1, tm)
    dst_tab = jnp.concatenate([spare, dst]).reshape(n_tiles + 1, 1, tm)
    return tile_expert, src_tab, dst_tab


def _pick(pref, n):
    return pref if n % pref == 0 else n


def kernel(x, p, ln_mix_g, ln_mix_b, ln_ffn_g, ln_ffn_b, fox_w_in, fox_b_f, fox_w_o, ssd_w_in, ssd_conv_w, ssd_conv_b, ssd_dt_bias, ssd_a_log, ssd_d, ssd_norm_g, ssd_w_out, ffn_w_gate, ffn_w_up, ffn_w_down, moe_router, moe_w_gate, moe_w_up, moe_w_down, ple_w_proj, ple_w_gate):
    bsz, seq, d = x.shape
    depth = p.shape[0]
    t = bsz * seq
    alpha = (2.0 * depth) ** 0.25
    x2 = x.reshape(t, d)
    p2 = p.reshape(depth, t, p.shape[-1])
    tm_big = _pick(1024, t)
    tm = _pick(512, t)
    ts = _pick(512, seq)
    one_tile = jnp.zeros((t // tm,), I32)
    all_valid = jnp.ones((t // tm,), I32)

    w_in = fox_w_in[0]
    qkv = _qkv_proj(x2, w_in[:, :3 * d].astype(BF16), tm_big)
    cum = _fgate_cumlog(x2, w_in[:, 3 * d:], fox_b_f[0], bsz, seq, ts)
    attn = _fox_attention(qkv, cum, bsz, seq, ts)
    h = _proj_residual_ln(attn, fox_w_o[0].astype(BF16), x2, ln_mix_g[0], ln_mix_b[0], alpha, tm,
                          "fox_out_proj_ln")
    y = _swiglu_tiles(h, ffn_w_gate.astype(BF16), ffn_w_up.astype(BF16), ffn_w_down.astype(BF16),
                      one_tile, all_valid, tm, 256, "ffn_swiglu_dense")
    h = _post_dense(h, y, p2[0], ln_ffn_g[0], ln_ffn_b[0], ple_w_proj[0].astype(BF16),
                    ple_w_gate[0].astype(BF16), alpha, tm)

    w_in = ssd_w_in[0]
    d_inner = ssd_norm_g.shape[1]
    conv_ch = ssd_conv_w.shape[2]
    z = _ssd_inproj(h, w_in[:, :d_inner].astype(BF16), tm_big, _pick(1024, d_inner), "ssd_inproj_z")
    xbc = _ssd_inproj_conv(h, w_in[:, d_inner:d_inner + conv_ch].astype(BF16), ssd_conv_w[0],
                           ssd_conv_b[0], seq, _pick(1024, seq), _pick(1024, conv_ch))
    dt, cs = _ssd_dt(h, w_in[:, d_inner + conv_ch:], ssd_dt_bias[0], ssd_a_log[0], ts)
    nh = ssd_dt_bias.shape[1]
    ymix = _ssd_scan(xbc, z, dt[:, :nh], cs[:, :nh], ssd_d[0], ssd_norm_g[0], bsz, seq, ts)
    h = _proj_residual_ln(ymix, ssd_w_out[0].astype(BF16), h, ln_mix_g[1], ln_mix_b[1], alpha, tm,
                          "ssd_out_proj_ln")
    ne = moe_router.shape[2]
    idx, gate_w, rank, counts = _moe_route(h, moe_router[0], tm)
    n_tiles = (TOP_K * t) // tm + ne
    tile_expert, src_tab, dst_tab = _group_plan(idx, rank, counts[:, 0], tm, n_tiles)
    y_tok = _moe_experts(h, moe_w_gate[0].astype(BF16), moe_w_up[0].astype(BF16),
                         moe_w_down[0].astype(BF16), tile_expert, src_tab, dst_tab,
                         TOP_K * t + tm, tm, 512)
    h = _post_moe(h, y_tok, gate_w.T, p2[1], ln_ffn_g[1], ln_ffn_b[1],
                  ple_w_proj[1].astype(BF16), ple_w_gate[1].astype(BF16), alpha, tm)
    return h.reshape(bsz, seq, d)
```

```python
import functools

import jax
import jax.numpy as jnp
from jax import lax
from jax.experimental import pallas as pl
from jax.experimental.pallas import tpu as pltpu

F32 = jnp.float32
BF16 = jnp.bfloat16
I32 = jnp.int32

LN_EPS = 1e-5
RMS_EPS = 1e-5
LANES = 128
HEAD_DIM = 64
SSD_GROUPS = 8
SSD_STATE = 128
SSD_CONV = 4
SSD_CHUNK = 128
TOP_K = 2
LOG2E = 1.4426950408889634
VMEM_LIMIT = 56 * 1024 * 1024


def _cp(sem, vmem=VMEM_LIMIT):
    return pltpu.CompilerParams(dimension_semantics=sem, vmem_limit_bytes=vmem)


def _dot(a, b):
    return jnp.dot(a, b, preferred_element_type=F32)


def _dot_nt(a, b):
    return lax.dot_general(a, b, (((1,), (1,)), ((), ())), preferred_element_type=F32)


def _dot_tn(a, b):
    return lax.dot_general(a, b, (((0,), (0,)), ((), ())), preferred_element_type=F32)


def _split2(v):
    hi = v.astype(BF16)
    lo = (v - hi.astype(F32)).astype(BF16)
    return hi, lo


def _split3(v):
    hi = v.astype(BF16)
    r = v - hi.astype(F32)
    mid = r.astype(BF16)
    lo = (r - mid.astype(F32)).astype(BF16)
    return hi, mid, lo


def _dot_x2w2(x, wh, wl):
    xh, xl = _split2(x)
    return _dot(xh, wh) + _dot(xl, wh) + _dot(xh, wl)


def _layer_norm(v, g, b):
    mu = jnp.mean(v, axis=-1, keepdims=True)
    vc = v - mu
    var = jnp.mean(vc * vc, axis=-1, keepdims=True)
    return vc * lax.rsqrt(var + LN_EPS) * g + b


def _silu(v):
    h = 0.5 * v
    return h + h * jnp.tanh(h)


def _softplus(v):
    return jnp.maximum(v, 0.0) + jnp.log1p(jnp.exp(-jnp.abs(v)))


def _qkv_kernel(x_ref, w_ref, o_ref, *, scale):
    j = pl.program_id(1)
    acc = _dot(x_ref[...].astype(BF16), w_ref[...])
    acc = acc * jnp.where(j == 0, scale, 1.0).astype(F32)
    for c in range(o_ref.shape[0]):
        o_ref[c] = acc[:, c * LANES:(c + 1) * LANES].astype(BF16)


def _qkv_proj(x, w_qkv, tm):
    t, d = x.shape
    n = w_qkv.shape[1]
    tn = d
    cpb = tn // LANES
    return pl.pallas_call(
        functools.partial(_qkv_kernel, scale=LOG2E * HEAD_DIM ** -0.5),
        grid=(t // tm, n // tn),
        in_specs=[pl.BlockSpec((tm, d), lambda i, j: (i, 0)),
                  pl.BlockSpec((d, tn), lambda i, j: (0, j))],
        out_specs=pl.BlockSpec((cpb, tm, LANES), lambda i, j: (j, i, 0)),
        out_shape=jax.ShapeDtypeStruct((n // LANES, t, LANES), BF16),
        compiler_params=_cp(("parallel", "arbitrary")),
        name="fox_qkv_proj",
    )(x, w_qkv)


def _fgate_kernel(x_ref, wh_ref, wl_ref, b_ref, o_ref, carry_ref):
    @pl.when(pl.program_id(1) == 0)
    def _():
        carry_ref[...] = jnp.zeros_like(carry_ref)

    z = _dot_x2w2(x_ref[...], wh_ref[...], wl_ref[...]) + b_ref[...]
    log_f = jnp.minimum(z, 0.0) - jnp.log1p(jnp.exp(-jnp.abs(z)))
    ts = log_f.shape[0]
    r = lax.broadcasted_iota(I32, (ts, ts), 0)
    c = lax.broadcasted_iota(I32, (ts, ts), 1)
    tri = jnp.where(r >= c, 1.0, 0.0).astype(BF16)
    hi, mid, lo = _split3(log_f)
    cs = _dot(tri, hi) + _dot(tri, mid) + _dot(tri, lo) + carry_ref[...]
    o_ref[...] = cs
    carry_ref[...] = cs[ts - 1:ts, :]


def _fgate_cumlog(x, w_f, b_f, bsz, seq, ts):
    t, d = x.shape
    nh = w_f.shape[1]
    w_pad = jnp.zeros((d, LANES), F32).at[:, :nh].set(w_f)
    wh, wl = _split2(w_pad)
    b_pad = jnp.zeros((1, LANES), F32).at[0, :nh].set(b_f)
    nsb = seq // ts
    return pl.pallas_call(
        _fgate_kernel,
        grid=(bsz, nsb),
        in_specs=[pl.BlockSpec((ts, d), lambda b, s: (b * nsb + s, 0)),
                  pl.BlockSpec((d, LANES), lambda b, s: (0, 0)),
                  pl.BlockSpec((d, LANES), lambda b, s: (0, 0)),
                  pl.BlockSpec((1, LANES), lambda b, s: (0, 0))],
        out_specs=pl.BlockSpec((ts, LANES), lambda b, s: (b * nsb + s, 0)),
        out_shape=jax.ShapeDtypeStruct((t, LANES), F32),
        scratch_shapes=[pltpu.VMEM((1, LANES), F32)],
        compiler_params=_cp(("parallel", "arbitrary")),
        name="fox_forget_cumlog",
    )(x, wh, wl, b_pad)


def _pieces3(v):
    hi = v.astype(BF16).astype(F32)
    r = v - hi
    mid = r.astype(BF16).astype(F32)
    lo = (r - mid).astype(BF16).astype(F32)
    return hi, mid, lo


AUG_ROWS = 16


def _aug_rows(rows):
    n = rows[0].shape[1]
    pad = [jnp.zeros((AUG_ROWS - len(rows), n), F32)]
    return jnp.concatenate(list(rows) + pad, axis=0).astype(BF16)


def _row_to_lane(base):
    r = lax.broadcasted_iota(I32, (AUG_ROWS, LANES), 0)
    lane = lax.broadcasted_iota(I32, (AUG_ROWS, LANES), 1)
    return jnp.where(lane == base + r, 1.0, 0.0).astype(BF16)


def _fox_kernel(q_ref, k_ref, v_ref, c_ref, o_ref, qa_ref, ka_ref, vt_ref, sa_ref, sb_ref, p_ref,
                m_ref, alpha_ref, acc_ref, *, tq, cb):
    qi = pl.program_id(2)
    seq = k_ref.shape[1]
    bases = (HEAD_DIM, 0)

    def data_lanes(lane, h):
        return (lane < HEAD_DIM) if h == 0 else (lane >= HEAD_DIM)

    lane = lax.broadcasted_iota(I32, (tq, LANES), 1)
    ones = jnp.ones((1, tq), F32)
    r128 = lax.broadcasted_iota(I32, (LANES, LANES), 0)
    c128 = lax.broadcasted_iota(I32, (LANES, LANES), 1)
    eye = jnp.where(r128 == c128, 1.0, 0.0).astype(BF16)

    @pl.when(qi == 0)
    def _():
        for c0 in range(0, seq, tq):
            rows = slice(c0, c0 + tq)
            k = k_ref[0, rows, :]
            v = v_ref[0, rows, :]
            hi, mid, lo = _pieces3(c_ref[0, 0, :, rows] * (-LOG2E))
            for h in range(2):
                k_rows = _aug_rows([ones, ones, ones, hi[h:h + 1], mid[h:h + 1], lo[h:h + 1]])
                k_aug = _dot_tn(k_rows, _row_to_lane(bases[h]))
                v_aug = jnp.where(lane == bases[h], 1.0, 0.0).astype(BF16)
                ka_ref[h, rows, :] = jnp.where(data_lanes(lane, h), k, k_aug.astype(BF16))
                vt_ref[h, :, rows] = _dot_nt(eye, jnp.where(data_lanes(lane, h), v, v_aug)).astype(BF16)

    q = q_ref[0]
    hi, mid, lo = _pieces3(c_ref[0, 0, :, pl.ds(pl.multiple_of(qi * tq, tq), tq)] * LOG2E)
    for h in range(2):
        q_rows = _aug_rows([hi[h:h + 1], mid[h:h + 1], lo[h:h + 1], ones, ones, ones])
        q_aug = _dot_tn(q_rows, _row_to_lane(bases[h]))
        qa_ref[h] = jnp.where(data_lanes(lane, h), q, q_aug.astype(BF16))
    m_ref[...] = jnp.full(m_ref.shape, -jnp.inf, F32)
    acc_ref[...] = jnp.zeros(acc_ref.shape, F32)

    def scores(j, s_ref):
        off = pl.multiple_of(j * tq, tq)
        for h in range(2):
            s_ref[h] = _dot_nt(ka_ref[h, pl.ds(off, tq), :], qa_ref[h])

    def softmax_pv(j, s_ref, diagonal):
        off = pl.multiple_of(j * tq, tq)
        for h in range(2):
            for c0 in range(0, tq, cb):
                cols = slice(c0, c0 + cb)
                nrow = c0 + cb if diagonal else tq
                s = s_ref[h, 0:nrow, cols]
                if diagonal:
                    key = lax.broadcasted_iota(I32, s.shape, 0)
                    qry = c0 + lax.broadcasted_iota(I32, s.shape, 1)
                    s = jnp.where(key <= qry, s, -jnp.inf)
                m_old = m_ref[h, :, cols]
                m_new = jnp.maximum(m_old, jnp.max(s, axis=0, keepdims=True))
                m_ref[h, :, cols] = m_new
                alpha_ref[h, :, cols] = jnp.exp2(m_old - m_new)
                p_ref[h, 0:nrow, cols] = jnp.exp2(s - m_new).astype(BF16)
                if nrow < tq:
                    p_ref[h, nrow:tq, cols] = jnp.zeros((tq - nrow, cb), BF16)
            acc_ref[h] = alpha_ref[h] * acc_ref[h] + _dot(vt_ref[h, :, pl.ds(off, tq)], p_ref[h])

    scores(0, sa_ref)

    def pair(i, carry):
        scores(2 * i + 1, sb_ref)
        softmax_pv(2 * i, sa_ref, False)
        scores(2 * i + 2, sa_ref)
        softmax_pv(2 * i + 1, sb_ref, False)
        return carry

    lax.fori_loop(0, qi // 2, pair, 0)

    @pl.when(qi % 2 == 0)
    def _():
        softmax_pv(qi, sa_ref, True)

    @pl.when(qi % 2 == 1)
    def _():
        scores(qi, sb_ref)
        softmax_pv(qi - 1, sa_ref, False)
        softmax_pv(qi, sb_ref, True)

    a0 = acc_ref[0]
    a1 = acc_ref[1]
    row = lax.broadcasted_iota(I32, a0.shape, 0)
    out_t = jnp.where(row < HEAD_DIM, a0 / a0[bases[0]:bases[0] + 1, :],
                      a1 / a1[bases[1]:bases[1] + 1, :])
    o_ref[...] = _dot_tn(out_t.astype(BF16), eye).astype(o_ref.dtype)


def _fox_attention(qkv, cum, bsz, seq, tq):
    hp = qkv.shape[0] // 3
    t = qkv.shape[1]
    c4 = cum[:, :2 * hp].reshape(bsz, seq, hp, 2).transpose(0, 2, 3, 1)
    nq = seq // tq
    return pl.pallas_call(
        functools.partial(_fox_kernel, tq=tq, cb=LANES),
        grid=(bsz, hp, nq),
        in_specs=[pl.BlockSpec((1, tq, LANES), lambda b, h, i: (h, b * nq + i, 0)),
                  pl.BlockSpec((1, seq, LANES), lambda b, h, i: (hp + h, b, 0)),
                  pl.BlockSpec((1, seq, LANES), lambda b, h, i: (2 * hp + h, b, 0)),
                  pl.BlockSpec((1, 1, 2, seq), lambda b, h, i: (b, h, 0, 0))],
        out_specs=pl.BlockSpec((tq, LANES), lambda b, h, i: (b * nq + i, h)),
        out_shape=jax.ShapeDtypeStruct((t, hp * LANES), BF16),
        scratch_shapes=[pltpu.VMEM((2, tq, LANES), BF16),
                        pltpu.VMEM((2, seq, LANES), BF16),
                        pltpu.VMEM((2, LANES, seq), BF16),
                        pltpu.VMEM((2, tq, tq), F32),
                        pltpu.VMEM((2, tq, tq), F32),
                        pltpu.VMEM((2, tq, tq), BF16),
                        pltpu.VMEM((2, 1, tq), F32),
                        pltpu.VMEM((2, 1, tq), F32),
                        pltpu.VMEM((2, LANES, tq), F32)],
        compiler_params=_cp(("parallel", "parallel", "arbitrary")),
        name="fox_flash_attention",
    )(qkv, qkv, qkv, c4)


def _proj_ln_kernel(a_ref, w_ref, res_ref, g_ref, b_ref, o_ref, *, alpha):
    y = _dot(a_ref[...], w_ref[...])
    o_ref[...] = _layer_norm(alpha * res_ref[...] + y, g_ref[...], b_ref[...])


def _proj_residual_ln(a, w, res, g, b, alpha, tm, name):
    t, k = a.shape
    d = w.shape[1]
    return pl.pallas_call(
        functools.partial(_proj_ln_kernel, alpha=alpha),
        grid=(t // tm,),
        in_specs=[pl.BlockSpec((tm, k), lambda i: (i, 0)),
                  pl.BlockSpec((k, d), lambda i: (0, 0)),
                  pl.BlockSpec((tm, d), lambda i: (i, 0)),
                  pl.BlockSpec((1, d), lambda i: (0, 0)),
                  pl.BlockSpec((1, d), lambda i: (0, 0))],
        out_specs=pl.BlockSpec((tm, d), lambda i: (i, 0)),
        out_shape=jax.ShapeDtypeStruct((t, d), F32),
        compiler_params=_cp(("parallel",)),
        name=name,
    )(a, w, res, g.reshape(1, d), b.reshape(1, d))


def _ple(h, p_ref, wp_ref, wgt_ref):
    gate = jax.nn.sigmoid(_dot(h.astype(BF16), wgt_ref[...]))
    proj = _dot(p_ref[...].astype(BF16), wp_ref[...])
    return h + proj * gate


def _ffn_dense_kernel(x_ref, wg_ref, wu_ref, wd_ref, p_ref, g_ref, b_ref, wp_ref, wgt_ref, o_ref,
                      acc_ref, *, fchunk, alpha):
    x = x_ref[...]
    xb = x.astype(BF16)
    f = wg_ref.shape[-1]
    for n, c0 in enumerate(range(0, f, fchunk)):
        c1 = min(c0 + fchunk, f)
        gate = _dot(xb, wg_ref[:, c0:c1])
        up = _dot(xb, wu_ref[:, c0:c1])
        y = _dot((_silu(gate) * up).astype(BF16), wd_ref[c0:c1, :])
        if n == 0:
            acc_ref[...] = y
        else:
            acc_ref[...] += y
    h = _layer_norm(alpha * x + acc_ref[...], g_ref[...], b_ref[...])
    o_ref[...] = _ple(h, p_ref, wp_ref, wgt_ref)


def _ffn_dense(x, w_gate, w_up, w_down, p, g, b, w_proj, w_gate_ple, alpha, tm, fchunk):
    t, d = x.shape
    f = w_gate.shape[1]
    pd = p.shape[1]
    row = lambda i: (i, 0)
    fixed = lambda i: (0, 0)
    once = pl.Buffered(1)
    return pl.pallas_call(
        functools.partial(_ffn_dense_kernel, fchunk=fchunk, alpha=alpha),
        grid=(t // tm,),
        in_specs=[pl.BlockSpec((tm, d), row),
                  pl.BlockSpec((d, f), fixed, pipeline_mode=once),
                  pl.BlockSpec((d, f), fixed, pipeline_mode=once),
                  pl.BlockSpec((f, d), fixed, pipeline_mode=once),
                  pl.BlockSpec((tm, pd), row),
                  pl.BlockSpec((1, d), fixed), pl.BlockSpec((1, d), fixed),
                  pl.BlockSpec((pd, d), fixed, pipeline_mode=once),
                  pl.BlockSpec((d, d), fixed, pipeline_mode=once)],
        out_specs=pl.BlockSpec((tm, d), row),
        out_shape=jax.ShapeDtypeStruct((t, d), F32),
        scratch_shapes=[pltpu.VMEM((tm, d), F32)],
        compiler_params=_cp(("parallel",)),
        name="ffn_swiglu_dense_post",
    )(x, w_gate, w_up, w_down, p, g.reshape(1, d), b.reshape(1, d), w_proj, w_gate_ple)


def _row_copy(src_hbm, row, dst, r, sem):
    return pltpu.make_async_copy(src_hbm.at[pl.ds(row, 1), :], dst.at[pl.ds(r, 1), :], sem)


def _post_moe_kernel(res_ref, y1_ref, y2_ref, gw_ref, p_ref, g_ref, b_ref, wp_ref, wgt_ref,
                     o_ref, *, alpha):
    gw = gw_ref[...]
    ffn = gw[:, 0:1] * y1_ref[...] + gw[:, 1:2] * y2_ref[...]
    h = _layer_norm(alpha * res_ref[...] + ffn, g_ref[...], b_ref[...])
    o_ref[...] = _ple(h, p_ref, wp_ref, wgt_ref)


def _post_moe(res, y_tok, gate_w, p, g, b, w_proj, w_gate, alpha, tm):
    t, d = res.shape
    pd = p.shape[1]
    nt = t // tm
    row = lambda i: (i, 0)
    fixed = lambda i: (0, 0)
    return pl.pallas_call(
        functools.partial(_post_moe_kernel, alpha=alpha),
        grid=(nt,),
        in_specs=[pl.BlockSpec((tm, d), row),
                  pl.BlockSpec((tm, d), row),
                  pl.BlockSpec((tm, d), lambda i: (nt + i, 0)),
                  pl.BlockSpec((tm, TOP_K), row),
                  pl.BlockSpec((tm, pd), row),
                  pl.BlockSpec((1, d), fixed), pl.BlockSpec((1, d), fixed),
                  pl.BlockSpec((pd, d), fixed), pl.BlockSpec((d, d), fixed)],
        out_specs=pl.BlockSpec((tm, d), row),
        out_shape=jax.ShapeDtypeStruct((t, d), F32),
        compiler_params=_cp(("parallel",)),
        name="post_ffn_moe_combine",
    )(res, y_tok, y_tok, gate_w, p, g.reshape(1, d), b.reshape(1, d), w_proj, w_gate)


def _moe_experts_kernel(te_ref, src_ref, nxt_ref, dst_ref, x_hbm, wg_ref, wu_ref, wd_ref, y_hbm,
                        xbuf0, xbuf1, ybuf0, ybuf1, xb_ref, gsem, ssem, *, fchunk):
    i = pl.program_id(0)
    last = te_ref[pl.num_programs(0)]
    tm = xbuf0.shape[0]
    xbufs = (xbuf0, xbuf1)
    ybufs = (ybuf0, ybuf1)

    def gather(row, buf, r, s):
        return _row_copy(x_hbm, row, buf, r, gsem.at[s])

    def scatter(buf, r, row, s):
        return pltpu.make_async_copy(buf.at[pl.ds(r, 1), :], y_hbm.at[pl.ds(row, 1), :], ssem.at[s])

    def wait_all(make):
        def body(r, _):
            make(r).wait()
            return 0
        lax.fori_loop(0, tm, body, 0, unroll=8)

    @pl.when(i == 0)
    def _():
        def body(r, _):
            gather(src_ref[0, 0, r], xbuf0, r, 0).start()
            return 0
        lax.fori_loop(0, tm, body, 0, unroll=8)
        ybuf1[...] = jnp.zeros_like(ybuf1)

    def step(s):
        o = 1 - s
        wait_all(lambda r: gather(0, xbufs[s], r, s))

        @pl.when(i >= 1)
        def _():
            wait_all(lambda r: scatter(ybufs[s], r, 0, s))

        for r in range(tm):
            gather(nxt_ref[0, 0, r], xbufs[o], r, o).start()
        for r in range(tm):
            scatter(ybufs[o], r, dst_ref[0, 0, r], o).start()
        xb_ref[...] = xbufs[s][...].astype(BF16)
        f = wg_ref.shape[-1]
        for n, c0 in enumerate(range(0, f, fchunk)):
            c1 = min(c0 + fchunk, f)
            g = _dot(xb_ref[...], wg_ref[0, :, c0:c1])
            u = _dot(xb_ref[...], wu_ref[0, :, c0:c1])
            y = _dot((_silu(g) * u).astype(BF16), wd_ref[0, c0:c1, :])
            if n == 0:
                ybufs[s][...] = y
            else:
                ybufs[s][...] += y

        @pl.when(i == last)
        def _():
            wait_all(lambda r: gather(0, xbufs[o], r, o))
            wait_all(lambda r: scatter(ybufs[o], r, 0, o))

    for s in range(2):
        pl.when(jnp.logical_and(i % 2 == s, i <= last))(functools.partial(step, s))


def _moe_experts(x, w_gate, w_up, w_down, tile_expert, src_tab, dst_tab, n_out_rows, tm, fchunk):
    d = x.shape[1]
    f = w_gate.shape[-1]
    steps = tile_expert.shape[0] - 1
    once = pl.Buffered(1)
    smem_tile = lambda index_map: pl.BlockSpec((1, 1, tm), index_map, memory_space=pltpu.SMEM)
    grid_spec = pltpu.PrefetchScalarGridSpec(
        num_scalar_prefetch=1,
        grid=(steps,),
        in_specs=[smem_tile(lambda i, te: (i, 0, 0)),
                  smem_tile(lambda i, te: (jnp.minimum(i + 1, steps - 1), 0, 0)),
                  smem_tile(lambda i, te: (i, 0, 0)),
                  pl.BlockSpec(memory_space=pl.ANY),
                  pl.BlockSpec((1, d, f), lambda i, te: (te[i], 0, 0), pipeline_mode=once),
                  pl.BlockSpec((1, d, f), lambda i, te: (te[i], 0, 0), pipeline_mode=once),
                  pl.BlockSpec((1, f, d), lambda i, te: (te[i], 0, 0), pipeline_mode=once)],
        out_specs=pl.BlockSpec(memory_space=pl.ANY),
        scratch_shapes=[pltpu.VMEM((tm, d), F32), pltpu.VMEM((tm, d), F32),
                        pltpu.VMEM((tm, d), F32), pltpu.VMEM((tm, d), F32),
                        pltpu.VMEM((tm, d), BF16),
                        pltpu.SemaphoreType.DMA((2,)), pltpu.SemaphoreType.DMA((2,))],
    )
    return pl.pallas_call(
        functools.partial(_moe_experts_kernel, fchunk=fchunk),
        grid_spec=grid_spec,
        out_shape=jax.ShapeDtypeStruct((n_out_rows, d), F32),
        compiler_params=_cp(("arbitrary",)),
        name="moe_experts_fused_dispatch",
    )(tile_expert, src_tab, src_tab, dst_tab, x, w_gate, w_up, w_down)


def _inproj_kernel(x_ref, w_ref, o_ref):
    o_ref[...] = _dot(x_ref[...].astype(BF16), w_ref[...]).astype(o_ref.dtype)


def _inproj_conv_kernel(x_ref, w_ref, cw_ref, cb_ref, o_ref, ext_ref, carry_ref, *, tiles_per_seq):
    i = pl.program_id(0)
    j = pl.program_id(1)
    tm = x_ref.shape[0]
    halo = carry_ref.shape[1]

    @pl.when(i % tiles_per_seq == 0)
    def _():
        carry_ref[j] = jnp.zeros(carry_ref.shape[1:], F32)

    acc = _dot(x_ref[...].astype(BF16), w_ref[...])
    ext_ref[0:halo, :] = carry_ref[j]
    ext_ref[halo:halo + tm, :] = acc
    cw = cw_ref[...]
    out = cb_ref[...] + cw[0:1, :] * ext_ref[halo - 3:halo - 3 + tm, :]
    for k in range(1, SSD_CONV):
        out = out + cw[k:k + 1, :] * ext_ref[halo - 3 + k:halo - 3 + k + tm, :]
    carry_ref[j] = acc[tm - halo:tm, :]
    o_ref[...] = _silu(out).astype(o_ref.dtype)


def _ssd_inproj(x, w, tm, tn, name):
    t, d = x.shape
    n = w.shape[1]
    return pl.pallas_call(
        _inproj_kernel,
        grid=(t // tm, n // tn),
        in_specs=[pl.BlockSpec((tm, d), lambda i, j: (i, 0)),
                  pl.BlockSpec((d, tn), lambda i, j: (0, j))],
        out_specs=pl.BlockSpec((tm, tn), lambda i, j: (i, j)),
        out_shape=jax.ShapeDtypeStruct((t, n), BF16),
        compiler_params=_cp(("parallel", "arbitrary")),
        name=name,
    )(x, w)


def _ssd_inproj_conv(x, w, conv_w, conv_b, seq, tm, tn):
    t, d = x.shape
    n = w.shape[1]
    halo = 8
    return pl.pallas_call(
        functools.partial(_inproj_conv_kernel, tiles_per_seq=seq // tm),
        grid=(t // tm, n // tn),
        in_specs=[pl.BlockSpec((tm, d), lambda i, j: (i, 0)),
                  pl.BlockSpec((d, tn), lambda i, j: (0, j)),
                  pl.BlockSpec((SSD_CONV, tn), lambda i, j: (0, j)),
                  pl.BlockSpec((1, tn), lambda i, j: (0, j))],
        out_specs=pl.BlockSpec((tm, tn), lambda i, j: (i, j)),
        out_shape=jax.ShapeDtypeStruct((t, n), BF16),
        scratch_shapes=[pltpu.VMEM((tm + halo, tn), F32), pltpu.VMEM((n // tn, halo, tn), F32)],
        compiler_params=_cp(("arbitrary", "arbitrary")),
        name="ssd_inproj_conv_silu",
    )(x, w, conv_w, conv_b.reshape(1, n))


def _ssd_dt_kernel(x_ref, wh_ref, wl_ref, bias_ref, alog_ref, dt_ref, cs_ref):
    dt = _softplus(_dot_x2w2(x_ref[...], wh_ref[...], wl_ref[...]) + bias_ref[...])
    da = dt * (-jnp.exp(alog_ref[...]))
    ts = da.shape[0]
    r = lax.broadcasted_iota(I32, (ts, ts), 0)
    c = lax.broadcasted_iota(I32, (ts, ts), 1)
    same_chunk = (r // SSD_CHUNK) == (c // SSD_CHUNK)
    tri = jnp.where(jnp.logical_and(r >= c, same_chunk), 1.0, 0.0).astype(BF16)
    hi, mid, lo = _split3(da)
    dt_ref[...] = dt
    cs_ref[...] = _dot(tri, hi) + _dot(tri, mid) + _dot(tri, lo)


def _ssd_dt(x, w_dt, dt_bias, a_log, ts):
    t, d = x.shape
    nh = w_dt.shape[1]
    w_pad = jnp.zeros((d, LANES), F32).at[:, :nh].set(w_dt)
    wh, wl = _split2(w_pad)
    bias = jnp.zeros((1, LANES), F32).at[0, :nh].set(dt_bias)
    alog = jnp.zeros((1, LANES), F32).at[0, :nh].set(a_log)
    fixed = lambda i: (0, 0)
    row = lambda i: (i, 0)
    return pl.pallas_call(
        _ssd_dt_kernel,
        grid=(t // ts,),
        in_specs=[pl.BlockSpec((ts, d), row), pl.BlockSpec((d, LANES), fixed),
                  pl.BlockSpec((d, LANES), fixed), pl.BlockSpec((1, LANES), fixed),
                  pl.BlockSpec((1, LANES), fixed)],
        out_specs=[pl.BlockSpec((ts, LANES), row), pl.BlockSpec((ts, LANES), row)],
        out_shape=[jax.ShapeDtypeStruct((t, LANES), F32), jax.ShapeDtypeStruct((t, LANES), F32)],
        compiler_params=_cp(("parallel",)),
        name="ssd_dt_prep",
    )(x, wh, wl, bias, alog)


def _ssd_scan_kernel(xs_ref, bm_ref, cm_ref, z_ref, dtt_ref, cst_ref, d_ref, ng_ref,
                     o_ref, h_ref):
    @pl.when(pl.program_id(2) == 0)
    def _():
        h_ref[...] = jnp.zeros_like(h_ref)

    lb, gw = xs_ref.shape
    hpg = gw // HEAD_DIM
    cl = SSD_CHUNK
    lane_head = lax.broadcasted_iota(I32, (cl, gw), 1) // HEAD_DIM
    r = lax.broadcasted_iota(I32, (cl, cl), 0)
    c = lax.broadcasted_iota(I32, (cl, cl), 1)
    causal = r >= c

    def spread(width, lanes_per_head):
        row = lax.broadcasted_iota(I32, (AUG_ROWS, width), 0)
        lane = lax.broadcasted_iota(I32, (AUG_ROWS, width), 1)
        head = row - (row // hpg) * hpg
        hit = jnp.logical_and(row < 3 * hpg, head == lane // lanes_per_head)
        return jnp.where(hit, 1.0, 0.0).astype(BF16)

    to_group = spread(gw, HEAD_DIM)
    to_blocks = spread(hpg * cl, cl)

    def piece_rows(v):
        pad = jnp.zeros((AUG_ROWS - 3 * hpg, v.shape[1]), F32)
        return jnp.concatenate(list(_pieces3(v)) + [pad], axis=0).astype(BF16)

    h = h_ref[...]
    for ci in range(lb // cl):
        sl = slice(ci * cl, (ci + 1) * cl)
        x = xs_ref[sl, :].astype(F32)
        bm = bm_ref[sl, :]
        cm = cm_ref[sl, :]
        cst = cst_ref[0, :, sl]
        cs_rows = piece_rows(cst)
        cs_e = _dot_tn(cs_rows, to_group)
        cs_b = _dot_tn(cs_rows, to_blocks)
        end_e = cs_e[cl - 1:cl, :]
        xdt = x * _dot_tn(piece_rows(dtt_ref[0, :, sl]), to_group)
        cb = _dot_nt(cm, bm)
        ms = []
        for hh in range(hpg):
            seg = cs_b[:, hh * cl:(hh + 1) * cl] - cst[hh:hh + 1, :]
            decay = jnp.exp(jnp.where(causal, seg, -jnp.inf))
            ms.append((cb * decay).astype(BF16))
        y_all = _dot(jnp.concatenate(ms, axis=0), xdt.astype(BF16))
        y = y_all[(hpg - 1) * cl:hpg * cl, :]
        for hh in range(hpg - 2, -1, -1):
            y = jnp.where(lane_head == hh, y_all[hh * cl:(hh + 1) * cl, :], y)
        y = y + _dot(cm, h.astype(BF16)) * jnp.exp(cs_e)
        y = y + d_ref[0] * x
        states = _dot_tn(bm, (xdt * jnp.exp(end_e - cs_e)).astype(BF16))
        h = jnp.exp(end_e) * h + states
        y = y * _silu(z_ref[sl, :].astype(F32))
        y = y * lax.rsqrt(jnp.mean(y * y, axis=-1, keepdims=True) + RMS_EPS)
        o_ref[sl, :] = (y * ng_ref[...]).astype(o_ref.dtype)
    h_ref[...] = h


def _ssd_scan(xbc, z, dt, cs, d_skip, norm_g, bsz, seq, lb):
    t = xbc.shape[0]
    d_inner = z.shape[1]
    g = SSD_GROUPS
    gw = d_inner // g
    hpg = gw // HEAD_DIM
    n = SSD_STATE
    nsb = seq // lb
    assert SSD_CHUNK == LANES and 3 * hpg <= AUG_ROWS
    dt_t = dt.reshape(t, g, hpg).transpose(1, 2, 0)
    cs_t = cs.reshape(t, g, hpg).transpose(1, 2, 0)
    d_e =jnp.repeat(d_skip.astype(F32), HEAD_DIM).reshape(g, 1, gw)
    rows = lambda b, gi, s: b * nsb + s
    return pl.pallas_call(
        _ssd_scan_kernel,
        grid=(bsz, g, nsb),
        in_specs=[pl.BlockSpec((lb, gw), lambda b, gi, s: (rows(b, gi, s), gi)),
                  pl.BlockSpec((lb, n), lambda b, gi, s: (rows(b, gi, s), d_inner // n + gi)),
                  pl.BlockSpec((lb, n), lambda b, gi, s: (rows(b, gi, s), d_inner // n + g + gi)),
                  pl.BlockSpec((lb, gw), lambda b, gi, s: (rows(b, gi, s), gi)),
                  pl.BlockSpec((1, hpg, lb), lambda b, gi, s: (gi, 0, rows(b, gi, s))),
                  pl.BlockSpec((1, hpg, lb), lambda b, gi, s: (gi, 0, rows(b, gi, s))),
                  pl.BlockSpec((1, 1, gw), lambda b, gi, s: (gi, 0, 0)),
                  pl.BlockSpec((1, gw), lambda b, gi, s: (0, gi))],
        out_specs=pl.BlockSpec((lb, gw), lambda b, gi, s: (rows(b, gi, s), gi)),
        out_shape=jax.ShapeDtypeStruct((t, d_inner), BF16),
        scratch_shapes=[pltpu.VMEM((n, gw), F32)],
        compiler_params=_cp(("parallel", "parallel", "arbitrary")),
        name="ssd_chunk_scan",
    )(xbc, xbc, xbc, z, dt_t, cs_t, d_e, norm_g.reshape(1, d_inner))


def _route_kernel(x_ref, rh_ref, rl_ref, idx_ref, gw_ref, rank_ref, cnt_ref, carry_ref):
    @pl.when(pl.program_id(0) == 0)
    def _():
        carry_ref[...] = jnp.zeros_like(carry_ref)

    xh, xl = _split2(x_ref[...])
    rh = rh_ref[...]
    logits = _dot_nt(rh, xh) + _dot_nt(rh, xl) + _dot_nt(rl_ref[...], xh)
    ne, tm = logits.shape
    e = lax.broadcasted_iota(I32, (ne, tm), 0)
    m1 = jnp.max(logits, axis=0, keepdims=True)
    i1 = jnp.min(jnp.where(logits == m1, e, ne), axis=0, keepdims=True)
    rest = jnp.where(e == i1, -jnp.inf, logits)
    m2 = jnp.max(rest, axis=0, keepdims=True)
    i2 = jnp.min(jnp.where(rest == m2, e, ne), axis=0, keepdims=True)
    ex = jnp.exp(m2 - m1)
    w1 = 1.0 / (1.0 + ex)
    w2 = ex / (1.0 + ex)
    sel1 = e == i1
    sel2 = e == i2
    sel = jnp.where(jnp.logical_or(sel1, sel2), 1.0, 0.0)
    r = lax.broadcasted_iota(I32, (tm, tm), 0)
    c = lax.broadcasted_iota(I32, (tm, tm), 1)
    before = jnp.where(r < c, 1.0, 0.0).astype(BF16)
    base = carry_ref[...][:, 0:1]
    rank = _dot(sel.astype(BF16), before) + base
    r1 = jnp.sum(jnp.where(sel1, rank, 0.0), axis=0, keepdims=True)
    r2 = jnp.sum(jnp.where(sel2, rank, 0.0), axis=0, keepdims=True)
    idx_ref[...] = jnp.concatenate([i1, i2], axis=0)
    gw_ref[...] = jnp.concatenate([w1, w2], axis=0)
    rank_ref[...] = jnp.concatenate([r1, r2], axis=0).astype(I32)
    total = carry_ref[...] + jnp.sum(sel, axis=1, keepdims=True)
    carry_ref[...] = total
    cnt_ref[...] = total.astype(I32)


def _moe_route(x, router, tm):
    t, d = x.shape
    ne = router.shape[1]
    rh, rl = _split2(router.T)
    fixed = lambda i: (0, 0)
    col = lambda i: (0, i)
    return pl.pallas_call(
        _route_kernel,
        grid=(t // tm,),
        in_specs=[pl.BlockSpec((tm, d), lambda i: (i, 0)),
                  pl.BlockSpec((ne, d), fixed), pl.BlockSpec((ne, d), fixed)],
        out_specs=[pl.BlockSpec((TOP_K, tm), col), pl.BlockSpec((TOP_K, tm), col),
                   pl.BlockSpec((TOP_K, tm), col), pl.BlockSpec((ne, LANES), fixed)],
        out_shape=[jax.ShapeDtypeStruct((TOP_K, t), I32), jax.ShapeDtypeStruct((TOP_K, t), F32),
                   jax.ShapeDtypeStruct((TOP_K, t), I32), jax.ShapeDtypeStruct((ne, LANES), I32)],
        scratch_shapes=[pltpu.VMEM((ne, LANES), F32)],
        compiler_params=_cp(("arbitrary",)),
        name="moe_route_top2",
    )(x, rh, rl)


def _group_plan(idx, rank, counts, tm, n_tiles):
    ne = counts.shape[0]
    t = idx.shape[1]
    tiles_e = (counts + tm - 1) // tm
    tile_end = jnp.cumsum(tiles_e)
    tile_start = tile_end - tiles_e
    start_of = jnp.sum(jnp.where(idx[:, :, None] == jnp.arange(ne, dtype=I32), tile_start, 0), axis=-1)
    slots = (start_of * tm + rank).reshape(-1)
    tile_ids = jnp.arange(n_tiles + 1, dtype=I32)
    tile_expert = jnp.minimum(jnp.searchsorted(tile_end, tile_ids, side="right"), ne - 1).astype(I32)
    tile_expert = jnp.concatenate([tile_expert, tile_end[ne - 1:].astype(I32)])
    spare = TOP_K * t + jnp.arange(tm, dtype=I32)
    dst = jnp.tile(spare, n_tiles).at[slots].set(jnp.arange(TOP_K * t, dtype=I32),
                                                 unique_indices=True, mode="promise_in_bounds")
    src = jnp.where(dst < TOP_K * t, dst % t, 0)
    src_tab = jnp.concatenate([src, jnp.zeros((tm,), I32)]).reshape(n_tiles + 1, 1, tm)
    dst_tab = jnp.concatenate([spare, dst]).reshape(n_tiles + 1, 1, tm)
    return tile_expert, src_tab, dst_tab


def _pick(pref, n):
    return pref if n % pref == 0 else n


def kernel(x, p, ln_mix_g, ln_mix_b, ln_ffn_g, ln_ffn_b, fox_w_in, fox_b_f, fox_w_o, ssd_w_in, ssd_conv_w, ssd_conv_b, ssd_dt_bias, ssd_a_log, ssd_d, ssd_norm_g, ssd_w_out, ffn_w_gate, ffn_w_up, ffn_w_down, moe_router, moe_w_gate, moe_w_up, moe_w_down, ple_w_proj, ple_w_gate):
    bsz, seq, d = x.shape
    depth = p.shape[0]
    t = bsz * seq
    alpha = (2.0 * depth) ** 0.25
    x2 = x.reshape(t, d)
    p2 = p.reshape(depth, t, p.shape[-1])
    tm_big = _pick(1024, t)
    tm = _pick(512, t)
    ts = _pick(512, seq)

    w_in = fox_w_in[0]
    qkv = _qkv_proj(x2, w_in[:, :3 * d].astype(BF16), tm_big)
    cum = _fgate_cumlog(x2, w_in[:, 3 * d:], fox_b_f[0], bsz, seq, ts)
    attn = _fox_attention(qkv, cum, bsz, seq, ts)
    h = _proj_residual_ln(attn, fox_w_o[0].astype(BF16), x2, ln_mix_g[0], ln_mix_b[0], alpha, tm,
                          "fox_out_proj_ln")
    h = _ffn_dense(h, ffn_w_gate[0].astype(BF16), ffn_w_up[0].astype(BF16),
                   ffn_w_down[0].astype(BF16), p2[0], ln_ffn_g[0], ln_ffn_b[0],
                   ple_w_proj[0].astype(BF16), ple_w_gate[0].astype(BF16), alpha, tm, 256)

    w_in = ssd_w_in[0]
    d_inner = ssd_norm_g.shape[1]
    conv_ch = ssd_conv_w.shape[2]
    z = _ssd_inproj(h, w_in[:, :d_inner].astype(BF16), tm_big, _pick(1024, d_inner), "ssd_inproj_z")
    xbc = _ssd_inproj_conv(h, w_in[:, d_inner:d_inner + conv_ch].astype(BF16), ssd_conv_w[0],
                           ssd_conv_b[0], seq, _pick(1024, seq), _pick(1024, conv_ch))
    dt, cs = _ssd_dt(h, w_in[:, d_inner + conv_ch:], ssd_dt_bias[0], ssd_a_log[0], ts)
    nh = ssd_dt_bias.shape[1]
    ymix = _ssd_scan(xbc, z, dt[:, :nh], cs[:, :nh], ssd_d[0], ssd_norm_g[0], bsz, seq, ts)
    h = _proj_residual_ln(ymix, ssd_w_out[0].astype(BF16), h, ln_mix_g[1], ln_mix_b[1], alpha, tm,
                          "ssd_out_proj_ln")
    ne = moe_router.shape[2]
    idx, gate_w, rank, counts = _moe_route(h, moe_router[0], tm)
    n_tiles = (TOP_K * t) // tm + ne
    tile_expert, src_tab, dst_tab = _group_plan(idx, rank, counts[:, 0], tm, n_tiles)
    y_tok = _moe_experts(h, moe_w_gate[0].astype(BF16), moe_w_up[0].astype(BF16),
                         moe_w_down[0].astype(BF16), tile_expert, src_tab, dst_tab,
                         TOP_K * t + tm, tm, 512)
    h = _post_moe(h, y_tok, gate_w.T, p2[1], ln_ffn_g[1], ln_ffn_b[1],
                  ple_w_proj[1].astype(BF16), ple_w_gate[1].astype(BF16), alpha, tm)
    return h.reshape(bsz, seq, d)
```

```python
import functools

import jax
import jax.numpy as jnp
from jax import lax
from jax.experimental import pallas as pl
from jax.experimental.pallas import tpu as pltpu

F32 = jnp.float32
BF16 = jnp.bfloat16
I32 = jnp.int32

LN_EPS = 1e-5
RMS_EPS = 1e-5
LANES = 128
HEAD_DIM = 64
SSD_GROUPS = 8
SSD_STATE = 128
SSD_CONV = 4
SSD_CHUNK = 128
TOP_K = 2
LOG2E = 1.4426950408889634
VMEM_LIMIT = 56 * 1024 * 1024


def _cp(sem, vmem=VMEM_LIMIT):
    return pltpu.CompilerParams(dimension_semantics=sem, vmem_limit_bytes=vmem)


def _dot(a, b):
    return jnp.dot(a, b, preferred_element_type=F32)


def _dot_nt(a, b):
    return lax.dot_general(a, b, (((1,), (1,)), ((), ())), preferred_element_type=F32)


def _dot_tn(a, b):
    return lax.dot_general(a, b, (((0,), (0,)), ((), ())), preferred_element_type=F32)


def _split2(v):
    hi = v.astype(BF16)
    lo = (v - hi.astype(F32)).astype(BF16)
    return hi, lo


def _split3(v):
    hi = v.astype(BF16)
    r = v - hi.astype(F32)
    mid = r.astype(BF16)
    lo = (r - mid.astype(F32)).astype(BF16)
    return hi, mid, lo


def _dot_x2w2(x, wh, wl):
    xh, xl = _split2(x)
    n = wh.shape[1]
    both = _dot(xh, jnp.concatenate([wh, wl], axis=1))
    return both[:, :n] + _dot(xl, wh) + both[:, n:]


def _chunk_cumsum(v, chunk):
    rows, n = v.shape
    r = lax.broadcasted_iota(I32, (chunk, chunk), 0)
    c = lax.broadcasted_iota(I32, (chunk, chunk), 1)
    tri = jnp.where(r >= c, 1.0, 0.0).astype(BF16)
    pieces = jnp.concatenate(_split3(v), axis=1)
    outs = []
    for c0 in range(0, rows, chunk):
        s3 = _dot(tri, pieces[c0:c0 + chunk, :])
        outs.append(s3[:, :n] + s3[:, n:2 * n] + s3[:, 2 * n:])
    return outs[0] if len(outs) == 1 else jnp.concatenate(outs, axis=0)


def _layer_norm(v, g, b):
    mu = jnp.mean(v, axis=-1, keepdims=True)
    vc = v - mu
    var = jnp.mean(vc * vc, axis=-1, keepdims=True)
    return vc * lax.rsqrt(var + LN_EPS) * g + b


def _silu(v):
    h = 0.5 * v
    return h + h * jnp.tanh(h)


def _softplus(v):
    return jnp.maximum(v, 0.0) + jnp.log1p(jnp.exp(-jnp.abs(v)))


def _qkv_kernel(x_ref, w_ref, o_ref, *, scale):
    j = pl.program_id(1)
    acc = _dot(x_ref[...].astype(BF16), w_ref[...])
    acc = acc * jnp.where(j == 0, scale, 1.0).astype(F32)
    for c in range(o_ref.shape[0]):
        o_ref[c] = acc[:, c * LANES:(c + 1) * LANES].astype(BF16)


def _qkv_proj(x, w_qkv, tm):
    t, d = x.shape
    n = w_qkv.shape[1]
    tn = d
    cpb = tn // LANES
    return pl.pallas_call(
        functools.partial(_qkv_kernel, scale=LOG2E * HEAD_DIM ** -0.5),
        grid=(t // tm, n // tn),
        in_specs=[pl.BlockSpec((tm, d), lambda i, j: (i, 0)),
                  pl.BlockSpec((d, tn), lambda i, j: (0, j))],
        out_specs=pl.BlockSpec((cpb, tm, LANES), lambda i, j: (j, i, 0)),
        out_shape=jax.ShapeDtypeStruct((n // LANES, t, LANES), BF16),
        compiler_params=_cp(("parallel", "arbitrary")),
        name="fox_qkv_proj",
    )(x, w_qkv)


def _fgate_kernel(x_ref, wh_ref, wl_ref, b_ref, o_ref, carry_ref):
    @pl.when(pl.program_id(1) == 0)
    def _():
        carry_ref[...] = jnp.zeros_like(carry_ref)

    z = _dot_x2w2(x_ref[...], wh_ref[...], wl_ref[...]) + b_ref[...]
    log_f = jnp.minimum(z, 0.0) - jnp.log1p(jnp.exp(-jnp.abs(z)))
    ts = log_f.shape[0]
    cs = _chunk_cumsum(log_f, ts) + carry_ref[...]
    o_ref[...] = cs
    carry_ref[...] = cs[ts - 1:ts, :]


def _fgate_cumlog(x, w_f, b_f, bsz, seq, ts):
    t, d = x.shape
    nh = w_f.shape[1]
    w_pad = jnp.zeros((d, LANES), F32).at[:, :nh].set(w_f)
    wh, wl = _split2(w_pad)
    b_pad = jnp.zeros((1, LANES), F32).at[0, :nh].set(b_f)
    nsb = seq // ts
    return pl.pallas_call(
        _fgate_kernel,
        grid=(bsz, nsb),
        in_specs=[pl.BlockSpec((ts, d), lambda b, s: (b * nsb + s, 0)),
                  pl.BlockSpec((d, LANES), lambda b, s: (0, 0)),
                  pl.BlockSpec((d, LANES), lambda b, s: (0, 0)),
                  pl.BlockSpec((1, LANES), lambda b, s: (0, 0))],
        out_specs=pl.BlockSpec((ts, LANES), lambda b, s: (b * nsb + s, 0)),
        out_shape=jax.ShapeDtypeStruct((t, LANES), F32),
        scratch_shapes=[pltpu.VMEM((1, LANES), F32)],
        compiler_params=_cp(("parallel", "arbitrary")),
        name="fox_forget_cumlog",
    )(x, wh, wl, b_pad)


def _pieces3(v):
    hi = v.astype(BF16).astype(F32)
    r = v - hi
    mid = r.astype(BF16).astype(F32)
    lo = (r - mid).astype(BF16).astype(F32)
    return hi, mid, lo


AUG_ROWS = 16


def _aug_rows(rows):
    n = rows[0].shape[1]
    pad = [jnp.zeros((AUG_ROWS - len(rows), n), F32)]
    return jnp.concatenate(list(rows) + pad, axis=0).astype(BF16)


def _row_to_lane(base):
    r = lax.broadcasted_iota(I32, (AUG_ROWS, LANES), 0)
    lane = lax.broadcasted_iota(I32, (AUG_ROWS, LANES), 1)
    return jnp.where(lane == base + r, 1.0, 0.0).astype(BF16)


def _fox_kernel(q_ref, k_ref, v_ref, c_ref, o_ref, qa_ref, ka_ref, vt_ref, sa_ref, sb_ref, p_ref,
                m_ref, alpha_ref, acc_ref, *, tq, cb):
    qi = pl.program_id(2)
    seq = k_ref.shape[1]
    bases = (HEAD_DIM, 0)

    def data_lanes(lane, h):
        return (lane < HEAD_DIM) if h == 0 else (lane >= HEAD_DIM)

    lane = lax.broadcasted_iota(I32, (tq, LANES), 1)
    ones = jnp.ones((1, tq), F32)
    r128 = lax.broadcasted_iota(I32, (LANES, LANES), 0)
    c128 = lax.broadcasted_iota(I32, (LANES, LANES), 1)
    eye = jnp.where(r128 == c128, 1.0, 0.0).astype(BF16)

    @pl.when(qi == 0)
    def _():
        for c0 in range(0, seq, tq):
            rows = slice(c0, c0 + tq)
            k = k_ref[0, rows, :]
            v = v_ref[0, rows, :]
            hi, mid, lo = _pieces3(c_ref[0, 0, :, rows] * (-LOG2E))
            for h in range(2):
                k_rows = _aug_rows([ones, ones, ones, hi[h:h + 1], mid[h:h + 1], lo[h:h + 1]])
                k_aug = _dot_tn(k_rows, _row_to_lane(bases[h]))
                v_aug = jnp.where(lane == bases[h], 1.0, 0.0).astype(BF16)
                ka_ref[h, rows, :] = jnp.where(data_lanes(lane, h), k, k_aug.astype(BF16))
                vt_ref[h, :, rows] = _dot_nt(eye, jnp.where(data_lanes(lane, h), v, v_aug)).astype(BF16)

    q = q_ref[0]
    hi, mid, lo = _pieces3(c_ref[0, 0, :, pl.ds(pl.multiple_of(qi * tq, tq), tq)] * LOG2E)
    for h in range(2):
        q_rows = _aug_rows([hi[h:h + 1], mid[h:h + 1], lo[h:h + 1], ones, ones, ones])
        q_aug = _dot_tn(q_rows, _row_to_lane(bases[h]))
        qa_ref[h] = jnp.where(data_lanes(lane, h), q, q_aug.astype(BF16))
    m_ref[...] = jnp.full(m_ref.shape, -jnp.inf, F32)
    acc_ref[...] = jnp.zeros(acc_ref.shape, F32)

    def scores(j, s_ref):
        off = pl.multiple_of(j * tq, tq)
        for h in range(2):
            s_ref[h] = _dot_nt(ka_ref[h, pl.ds(off, tq), :], qa_ref[h])

    def softmax_pv(j, s_ref, diagonal):
        off = pl.multiple_of(j * tq, tq)
        for h in range(2):
            for c0 in range(0, tq, cb):
                cols = slice(c0, c0 + cb)
                nrow = c0 + cb if diagonal else tq
                s = s_ref[h, 0:nrow, cols]
                if diagonal:
                    key = lax.broadcasted_iota(I32, s.shape, 0)
                    qry = c0 + lax.broadcasted_iota(I32, s.shape, 1)
                    s = jnp.where(key <= qry, s, -jnp.inf)
                m_old = m_ref[h, :, cols]
                m_new = jnp.maximum(m_old, jnp.max(s, axis=0, keepdims=True))
                m_ref[h, :, cols] = m_new
                alpha_ref[h, :, cols] = jnp.exp2(m_old - m_new)
                p_ref[h, 0:nrow, cols] = jnp.exp2(s - m_new).astype(BF16)
                if nrow < tq:
                    p_ref[h, nrow:tq, cols] = jnp.zeros((tq - nrow, cb), BF16)
            acc_ref[h] = alpha_ref[h] * acc_ref[h] + _dot(vt_ref[h, :, pl.ds(off, tq)], p_ref[h])

    scores(0, sa_ref)

    def pair(i, carry):
        scores(2 * i + 1, sb_ref)
        softmax_pv(2 * i, sa_ref, False)
        scores(2 * i + 2, sa_ref)
        softmax_pv(2 * i + 1, sb_ref, False)
        return carry

    lax.fori_loop(0, qi // 2, pair, 0)

    @pl.when(qi % 2 == 0)
    def _():
        softmax_pv(qi, sa_ref, True)

    @pl.when(qi % 2 == 1)
    def _():
        scores(qi, sb_ref)
        softmax_pv(qi - 1, sa_ref, False)
        softmax_pv(qi, sb_ref, True)

    a0 = acc_ref[0]
    a1 = acc_ref[1]
    row = lax.broadcasted_iota(I32, a0.shape, 0)
    out_t = jnp.where(row < HEAD_DIM, a0 / a0[bases[0]:bases[0] + 1, :],
                      a1 / a1[bases[1]:bases[1] + 1, :])
    o_ref[...] = _dot_tn(out_t.astype(BF16), eye).astype(o_ref.dtype)


def _fox_attention(qkv, cum, bsz, seq, tq):
    hp = qkv.shape[0] // 3
    t = qkv.shape[1]
    c4 = cum[:, :2 * hp].reshape(bsz, seq, hp, 2).transpose(0, 2, 3, 1)
    nq = seq // tq
    return pl.pallas_call(
        functools.partial(_fox_kernel, tq=tq, cb=LANES),
        grid=(bsz, hp, nq),
        in_specs=[pl.BlockSpec((1, tq, LANES), lambda b, h, i: (h, b * nq + i, 0)),
                  pl.BlockSpec((1, seq, LANES), lambda b, h, i: (hp + h, b, 0)),
                  pl.BlockSpec((1, seq, LANES), lambda b, h, i: (2 * hp + h, b, 0)),
                  pl.BlockSpec((1, 1, 2, seq), lambda b, h, i: (b, h, 0, 0))],
        out_specs=pl.BlockSpec((tq, LANES), lambda b, h, i: (b * nq + i, h)),
        out_shape=jax.ShapeDtypeStruct((t, hp * LANES), BF16),
        scratch_shapes=[pltpu.VMEM((2, tq, LANES), BF16),
                        pltpu.VMEM((2, seq, LANES), BF16),
                        pltpu.VMEM((2, LANES, seq), BF16),
                        pltpu.VMEM((2, tq, tq), F32),
                        pltpu.VMEM((2, tq, tq), F32),
                        pltpu.VMEM((2, tq, tq), BF16),
                        pltpu.VMEM((2, 1, tq), F32),
                        pltpu.VMEM((2, 1, tq), F32),
                        pltpu.VMEM((2, LANES, tq), F32)],
        compiler_params=_cp(("parallel", "parallel", "arbitrary")),
        name="fox_flash_attention",
    )(qkv, qkv, qkv, c4)


def _proj_ln_kernel(a_ref, w_ref, res_ref, g_ref, b_ref, o_ref, *, alpha):
    y = _dot(a_ref[...], w_ref[...])
    o_ref[...] = _layer_norm(alpha * res_ref[...] + y, g_ref[...], b_ref[...])


def _proj_residual_ln(a, w, res, g, b, alpha, tm, name):
    t, k = a.shape
    d = w.shape[1]
    return pl.pallas_call(
        functools.partial(_proj_ln_kernel, alpha=alpha),
        grid=(t // tm,),
        in_specs=[pl.BlockSpec((tm, k), lambda i: (i, 0)),
                  pl.BlockSpec((k, d), lambda i: (0, 0)),
                  pl.BlockSpec((tm, d), lambda i: (i, 0)),
                  pl.BlockSpec((1, d), lambda i: (0, 0)),
                  pl.BlockSpec((1, d), lambda i: (0, 0))],
        out_specs=pl.BlockSpec((tm, d), lambda i: (i, 0)),
        out_shape=jax.ShapeDtypeStruct((t, d), F32),
        compiler_params=_cp(("parallel",)),
        name=name,
    )(a, w, res, g.reshape(1, d), b.reshape(1, d))


def _ple(h, p_ref, wp_ref, wgt_ref):
    gate = jax.nn.sigmoid(_dot(h.astype(BF16), wgt_ref[...]))
    proj = _dot(p_ref[...].astype(BF16), wp_ref[...])
    return h + proj * gate


def _ffn_dense_kernel(x_ref, wg_ref, wu_ref, wd_ref, p_ref, g_ref, b_ref, wp_ref, wgt_ref, o_ref,
                      acc_ref, *, fchunk, alpha):
    x = x_ref[...]
    xb = x.astype(BF16)
    f = wg_ref.shape[-1]
    for n, c0 in enumerate(range(0, f, fchunk)):
        c1 = min(c0 + fchunk, f)
        gate = _dot(xb, wg_ref[:, c0:c1])
        up = _dot(xb, wu_ref[:, c0:c1])
        y = _dot((_silu(gate) * up).astype(BF16), wd_ref[c0:c1, :])
        if n == 0:
            acc_ref[...] = y
        else:
            acc_ref[...] += y
    h = _layer_norm(alpha * x + acc_ref[...], g_ref[...], b_ref[...])
    o_ref[...] = _ple(h, p_ref, wp_ref, wgt_ref)


def _ffn_dense(x, w_gate, w_up, w_down, p, g, b, w_proj, w_gate_ple, alpha, tm, fchunk):
    t, d = x.shape
    f = w_gate.shape[1]
    pd = p.shape[1]
    row = lambda i: (i, 0)
    fixed = lambda i: (0, 0)
    once = pl.Buffered(1)
    return pl.pallas_call(
        functools.partial(_ffn_dense_kernel, fchunk=fchunk, alpha=alpha),
        grid=(t // tm,),
        in_specs=[pl.BlockSpec((tm, d), row),
                  pl.BlockSpec((d, f), fixed, pipeline_mode=once),
                  pl.BlockSpec((d, f), fixed, pipeline_mode=once),
                  pl.BlockSpec((f, d), fixed, pipeline_mode=once),
                  pl.BlockSpec((tm, pd), row),
                  pl.BlockSpec((1, d), fixed), pl.BlockSpec((1, d), fixed),
                  pl.BlockSpec((pd, d), fixed, pipeline_mode=once),
                  pl.BlockSpec((d, d), fixed, pipeline_mode=once)],
        out_specs=pl.BlockSpec((tm, d), row),
        out_shape=jax.ShapeDtypeStruct((t, d), F32),
        scratch_shapes=[pltpu.VMEM((tm, d), F32)],
        compiler_params=_cp(("parallel",)),
        name="ffn_swiglu_dense_post",
    )(x, w_gate, w_up, w_down, p, g.reshape(1, d), b.reshape(1, d), w_proj, w_gate_ple)


def _row_copy(src_hbm, row, dst, r, sem):
    return pltpu.make_async_copy(src_hbm.at[pl.ds(row, 1), :], dst.at[pl.ds(r, 1), :], sem)


def _post_moe_kernel(res_ref, y1_ref, y2_ref, gw_ref, p_ref, g_ref, b_ref, wp_ref, wgt_ref,
                     o_ref, *, alpha):
    gw = gw_ref[...]
    ffn = gw[:, 0:1] * y1_ref[...] + gw[:, 1:2] * y2_ref[...]
    h = _layer_norm(alpha * res_ref[...] + ffn, g_ref[...], b_ref[...])
    o_ref[...] = _ple(h, p_ref, wp_ref, wgt_ref)


def _post_moe(res, y_tok, gate_w, p, g, b, w_proj, w_gate, alpha, tm):
    t, d = res.shape
    pd = p.shape[1]
    nt = t // tm
    row = lambda i: (i, 0)
    fixed = lambda i: (0, 0)
    return pl.pallas_call(
        functools.partial(_post_moe_kernel, alpha=alpha),
        grid=(nt,),
        in_specs=[pl.BlockSpec((tm, d), row),
                  pl.BlockSpec((tm, d), row),
                  pl.BlockSpec((tm, d), lambda i: (nt + i, 0)),
                  pl.BlockSpec((tm, TOP_K), row),
                  pl.BlockSpec((tm, pd), row),
                  pl.BlockSpec((1, d), fixed), pl.BlockSpec((1, d), fixed),
                  pl.BlockSpec((pd, d), fixed), pl.BlockSpec((d, d), fixed)],
        out_specs=pl.BlockSpec((tm, d), row),
        out_shape=jax.ShapeDtypeStruct((t, d), F32),
        compiler_params=_cp(("parallel",)),
        name="post_ffn_moe_combine",
    )(res, y_tok, y_tok, gate_w, p, g.reshape(1, d), b.reshape(1, d), w_proj, w_gate)


def _moe_experts_kernel(te_ref, src_ref, nxt_ref, dst_ref, x_hbm, wg_ref, wu_ref, wd_ref, y_hbm,
                        xbuf0, xbuf1, ybuf0, ybuf1, xb_ref, gsem, ssem, *, fchunk):
    i = pl.program_id(0)
    last = te_ref[pl.num_programs(0)]
    tm = xbuf0.shape[0]
    xbufs = (xbuf0, xbuf1)
    ybufs = (ybuf0, ybuf1)

    def gather(row, buf, r, s):
        return _row_copy(x_hbm, row, buf, r, gsem.at[s])

    def scatter(buf, r, row, s):
        return pltpu.make_async_copy(buf.at[pl.ds(r, 1), :], y_hbm.at[pl.ds(row, 1), :], ssem.at[s])

    def wait_all(make):
        def body(r, _):
            make(r).wait()
            return 0
        lax.fori_loop(0, tm, body, 0, unroll=8)

    @pl.when(i == 0)
    def _():
        def body(r, _):
            gather(src_ref[0, 0, r], xbuf0, r, 0).start()
            return 0
        lax.fori_loop(0, tm, body, 0, unroll=8)
        ybuf1[...] = jnp.zeros_like(ybuf1)

    def step(s):
        o = 1 - s
        wait_all(lambda r: gather(0, xbufs[s], r, s))

        @pl.when(i >= 1)
        def _():
            wait_all(lambda r: scatter(ybufs[s], r, 0, s))

        xb_ref[...] = xbufs[s][...].astype(BF16)
        f = wg_ref.shape[-1]
        starts = list(range(0, f, fchunk))
        per = -(-tm // len(starts))
        after = jnp.int32(0)
        for n, c0 in enumerate(starts):
            for r in range(n * per, min((n + 1) * per, tm)):
                gather(nxt_ref[0, 0, r] + after, xbufs[o], r, o).start()
                scatter(ybufs[o], r, dst_ref[0, 0, r] + after, o).start()
            c1 = min(c0 + fchunk, f)
            g = _dot(xb_ref[...], wg_ref[0, :, c0:c1])
            u = _dot(xb_ref[...], wu_ref[0, :, c0:c1])
            y = _dot((_silu(g) * u).astype(BF16), wd_ref[0, c0:c1, :])
            if n == 0:
                ybufs[s][...] = y
            else:
                ybufs[s][...] += y
            after = (jnp.max(y[0:8, 0:LANES]) > jnp.inf).astype(I32)

        @pl.when(i == last)
        def _():
            wait_all(lambda r: gather(0, xbufs[o], r, o))
            wait_all(lambda r: scatter(ybufs[o], r, 0, o))

    for s in range(2):
        pl.when(jnp.logical_and(i % 2 == s, i <= last))(functools.partial(step, s))


def _moe_experts(x, w_gate, w_up, w_down, tile_expert, src_tab, dst_tab, n_out_rows, tm, fchunk):
    d = x.shape[1]
    f = w_gate.shape[-1]
    steps = tile_expert.shape[0] - 1
    once = pl.Buffered(1)
    smem_tile = lambda index_map: pl.BlockSpec((1, 1, tm), index_map, memory_space=pltpu.SMEM)
    grid_spec = pltpu.PrefetchScalarGridSpec(
        num_scalar_prefetch=1,
        grid=(steps,),
        in_specs=[smem_tile(lambda i, te: (i, 0, 0)),
                  smem_tile(lambda i, te: (jnp.minimum(i + 1, steps - 1), 0, 0)),
                  smem_tile(lambda i, te: (i, 0, 0)),
                  pl.BlockSpec(memory_space=pl.ANY),
                  pl.BlockSpec((1, d, f), lambda i, te: (te[i], 0, 0), pipeline_mode=once),
                  pl.BlockSpec((1, d, f), lambda i, te: (te[i], 0, 0), pipeline_mode=once),
                  pl.BlockSpec((1, f, d), lambda i, te: (te[i], 0, 0), pipeline_mode=once)],
        out_specs=pl.BlockSpec(memory_space=pl.ANY),
        scratch_shapes=[pltpu.VMEM((tm, d), F32), pltpu.VMEM((tm, d), F32),
                        pltpu.VMEM((tm, d), F32), pltpu.VMEM((tm, d), F32),
                        pltpu.VMEM((tm, d), BF16),
                        pltpu.SemaphoreType.DMA((2,)), pltpu.SemaphoreType.DMA((2,))],
    )
    return pl.pallas_call(
        functools.partial(_moe_experts_kernel, fchunk=fchunk),
        grid_spec=grid_spec,
        out_shape=jax.ShapeDtypeStruct((n_out_rows, d), F32),
        compiler_params=_cp(("arbitrary",)),
        name="moe_experts_fused_dispatch",
    )(tile_expert, src_tab, src_tab, dst_tab, x, w_gate, w_up, w_down)


def _inproj_kernel(x_ref, w_ref, o_ref):
    o_ref[...] = _dot(x_ref[...].astype(BF16), w_ref[...]).astype(o_ref.dtype)


def _inproj_conv_kernel(x_ref, w_ref, cw_ref, cb_ref, o_ref, ext_ref, carry_ref, *, tiles_per_seq):
    i = pl.program_id(0)
    j = pl.program_id(1)
    tm = x_ref.shape[0]
    halo = carry_ref.shape[1]

    @pl.when(i % tiles_per_seq == 0)
    def _():
        carry_ref[j] = jnp.zeros(carry_ref.shape[1:], F32)

    acc = _dot(x_ref[...].astype(BF16), w_ref[...])
    ext_ref[0:halo, :] = carry_ref[j]
    ext_ref[halo:halo + tm, :] = acc
    cw = cw_ref[...]
    out = cb_ref[...] + cw[0:1, :] * ext_ref[halo - 3:halo - 3 + tm, :]
    for k in range(1, SSD_CONV):
        out = out + cw[k:k + 1, :] * ext_ref[halo - 3 + k:halo - 3 + k + tm, :]
    carry_ref[j] = acc[tm - halo:tm, :]
    o_ref[...] = _silu(out).astype(o_ref.dtype)


def _ssd_inproj(x, w, tm, tn, name):
    t, d = x.shape
    n = w.shape[1]
    return pl.pallas_call(
        _inproj_kernel,
        grid=(t // tm, n // tn),
        in_specs=[pl.BlockSpec((tm, d), lambda i, j: (i, 0)),
                  pl.BlockSpec((d, tn), lambda i, j: (0, j))],
        out_specs=pl.BlockSpec((tm, tn), lambda i, j: (i, j)),
        out_shape=jax.ShapeDtypeStruct((t, n), BF16),
        compiler_params=_cp(("parallel", "arbitrary")),
        name=name,
    )(x, w)


def _ssd_inproj_conv(x, w, conv_w, conv_b, seq, tm, tn):
    t, d = x.shape
    n = w.shape[1]
    halo = 8
    return pl.pallas_call(
        functools.partial(_inproj_conv_kernel, tiles_per_seq=seq // tm),
        grid=(t // tm, n // tn),
        in_specs=[pl.BlockSpec((tm, d), lambda i, j: (i, 0)),
                  pl.BlockSpec((d, tn), lambda i, j: (0, j)),
                  pl.BlockSpec((SSD_CONV, tn), lambda i, j: (0, j)),
                  pl.BlockSpec((1, tn), lambda i, j: (0, j))],
        out_specs=pl.BlockSpec((tm, tn), lambda i, j: (i, j)),
        out_shape=jax.ShapeDtypeStruct((t, n), BF16),
        scratch_shapes=[pltpu.VMEM((tm + halo, tn), F32), pltpu.VMEM((n // tn, halo, tn), F32)],
        compiler_params=_cp(("arbitrary", "arbitrary")),
        name="ssd_inproj_conv_silu",
    )(x, w, conv_w, conv_b.reshape(1, n))


def _ssd_dt_kernel(x_ref, wh_ref, wl_ref, bias_ref, alog_ref, dt_ref, cs_ref):
    dt = _softplus(_dot_x2w2(x_ref[...], wh_ref[...], wl_ref[...]) + bias_ref[...])
    da = dt * (-jnp.exp(alog_ref[...]))
    dt_ref[...] = dt
    cs_ref[...] = _chunk_cumsum(da, SSD_CHUNK)


def _ssd_dt(x, w_dt, dt_bias, a_log, ts):
    t, d = x.shape
    nh = w_dt.shape[1]
    w_pad = jnp.zeros((d, LANES), F32).at[:, :nh].set(w_dt)
    wh, wl = _split2(w_pad)
    bias = jnp.zeros((1, LANES), F32).at[0, :nh].set(dt_bias)
    alog = jnp.zeros((1, LANES), F32).at[0, :nh].set(a_log)
    fixed = lambda i: (0, 0)
    row = lambda i: (i, 0)
    return pl.pallas_call(
        _ssd_dt_kernel,
        grid=(t // ts,),
        in_specs=[pl.BlockSpec((ts, d), row), pl.BlockSpec((d, LANES), fixed),
                  pl.BlockSpec((d, LANES), fixed), pl.BlockSpec((1, LANES), fixed),
                  pl.BlockSpec((1, LANES), fixed)],
        out_specs=[pl.BlockSpec((ts, LANES), row), pl.BlockSpec((ts, LANES), row)],
        out_shape=[jax.ShapeDtypeStruct((t, LANES), F32), jax.ShapeDtypeStruct((t, LANES), F32)],
        compiler_params=_cp(("parallel",)),
        name="ssd_dt_prep",
    )(x, wh, wl, bias, alog)


def _ssd_scan_kernel(xs_ref, bm_ref, cm_ref, z_ref, dtt_ref, cst_ref, d_ref, ng_ref,
                     o_ref, h_ref):
    @pl.when(pl.program_id(2) == 0)
    def _():
        h_ref[...] = jnp.zeros_like(h_ref)

    lb, gw = xs_ref.shape
    hpg = gw // HEAD_DIM
    cl = SSD_CHUNK
    lane_head = lax.broadcasted_iota(I32, (cl, gw), 1) // HEAD_DIM
    r = lax.broadcasted_iota(I32, (cl, cl), 0)
    c = lax.broadcasted_iota(I32, (cl, cl), 1)
    causal = r >= c

    def spread(width, lanes_per_head):
        row = lax.broadcasted_iota(I32, (AUG_ROWS, width), 0)
        lane = lax.broadcasted_iota(I32, (AUG_ROWS, width), 1)
        head = row - (row // hpg) * hpg
        hit = jnp.logical_and(row < 3 * hpg, head == lane // lanes_per_head)
        return jnp.where(hit, 1.0, 0.0).astype(BF16)

    to_group = spread(gw, HEAD_DIM)
    to_blocks = spread(hpg * cl, cl)

    def piece_rows(v):
        pad = jnp.zeros((AUG_ROWS - 3 * hpg, v.shape[1]), F32)
        return jnp.concatenate(list(_pieces3(v)) + [pad], axis=0).astype(BF16)

    h = h_ref[...]
    for ci in range(lb // cl):
        sl = slice(ci * cl, (ci + 1) * cl)
        x = xs_ref[sl, :].astype(F32)
        bm = bm_ref[sl, :]
        cm = cm_ref[sl, :]
        cst = cst_ref[0, :, sl]
        cs_rows = piece_rows(cst)
        cs_e = _dot_tn(cs_rows, to_group)
        cs_b = _dot_tn(cs_rows, to_blocks)
        end_e = cs_e[cl - 1:cl, :]
        xdt = x * _dot_tn(piece_rows(dtt_ref[0, :, sl]), to_group)
        cb = _dot_nt(cm, bm)
        ms = []
        for hh in range(hpg):
            seg = cs_b[:, hh * cl:(hh + 1) * cl] - cst[hh:hh + 1, :]
            decay = jnp.exp(jnp.where(causal, seg, -jnp.inf))
            ms.append((cb * decay).astype(BF16))
        y_all = _dot(jnp.concatenate(ms, axis=0), xdt.astype(BF16))
        y = y_all[(hpg - 1) * cl:hpg * cl, :]
        for hh in range(hpg - 2, -1, -1):
            y = jnp.where(lane_head == hh, y_all[hh * cl:(hh + 1) * cl, :], y)
        y = y + _dot(cm, h.astype(BF16)) * jnp.exp(cs_e)
        y = y + d_ref[0] * x
        states = _dot_tn(bm, (xdt * jnp.exp(end_e - cs_e)).astype(BF16))
        h = jnp.exp(end_e) * h + states
        y = y * _silu(z_ref[sl, :].astype(F32))
        y = y * lax.rsqrt(jnp.mean(y * y, axis=-1, keepdims=True) + RMS_EPS)
        o_ref[sl, :] = (y * ng_ref[...]).astype(o_ref.dtype)
    h_ref[...] = h


def _ssd_scan(xbc, z, dt, cs, d_skip, norm_g, bsz, seq, lb):
    t = xbc.shape[0]
    d_inner = z.shape[1]
    g = SSD_GROUPS
    gw = d_inner // g
    hpg = gw // HEAD_DIM
    n = SSD_STATE
    nsb = seq // lb
    assert SSD_CHUNK == LANES and 3 * hpg <= AUG_ROWS
    dt_t = dt.reshape(t, g, hpg).transpose(1, 2, 0)
    cs_t = cs.reshape(t, g, hpg).transpose(1, 2, 0)
    d_e =jnp.repeat(d_skip.astype(F32), HEAD_DIM).reshape(g, 1, gw)
    rows = lambda b, gi, s: b * nsb + s
    return pl.pallas_call(
        _ssd_scan_kernel,
        grid=(bsz, g, nsb),
        in_specs=[pl.BlockSpec((lb, gw), lambda b, gi, s: (rows(b, gi, s), gi)),
                  pl.BlockSpec((lb, n), lambda b, gi, s: (rows(b, gi, s), d_inner // n + gi)),
                  pl.BlockSpec((lb, n), lambda b, gi, s: (rows(b, gi, s), d_inner // n + g + gi)),
                  pl.BlockSpec((lb, gw), lambda b, gi, s: (rows(b, gi, s), gi)),
                  pl.BlockSpec((1, hpg, lb), lambda b, gi, s: (gi, 0, rows(b, gi, s))),
                  pl.BlockSpec((1, hpg, lb), lambda b, gi, s: (gi, 0, rows(b, gi, s))),
                  pl.BlockSpec((1, 1, gw), lambda b, gi, s: (gi, 0, 0)),
                  pl.BlockSpec((1, gw), lambda b, gi, s: (0, gi))],
        out_specs=pl.BlockSpec((lb, gw), lambda b, gi, s: (rows(b, gi, s), gi)),
        out_shape=jax.ShapeDtypeStruct((t, d_inner), BF16),
        scratch_shapes=[pltpu.VMEM((n, gw), F32)],
        compiler_params=_cp(("parallel", "parallel", "arbitrary")),
        name="ssd_chunk_scan",
    )(xbc, xbc, xbc, z, dt_t, cs_t, d_e, norm_g.reshape(1, d_inner))


def _route_kernel(x_ref, rh_ref, rl_ref, idx_ref, gw_ref, rank_ref, cnt_ref, carry_ref):
    @pl.when(pl.program_id(0) == 0)
    def _():
        carry_ref[...] = jnp.zeros_like(carry_ref)

    xh, xl = _split2(x_ref[...])
    rh = rh_ref[...]
    logits = _dot_nt(rh, xh) + _dot_nt(rh, xl) + _dot_nt(rl_ref[...], xh)
    ne, tm = logits.shape
    e = lax.broadcasted_iota(I32, (ne, tm), 0)
    m1 = jnp.max(logits, axis=0, keepdims=True)
    i1 = jnp.min(jnp.where(logits == m1, e, ne), axis=0, keepdims=True)
    rest = jnp.where(e == i1, -jnp.inf, logits)
    m2 = jnp.max(rest, axis=0, keepdims=True)
    i2 = jnp.min(jnp.where(rest == m2, e, ne), axis=0, keepdims=True)
    ex = jnp.exp(m2 - m1)
    w1 = 1.0 / (1.0 + ex)
    w2 = ex / (1.0 + ex)
    sel1 = e == i1
    sel2 = e == i2
    sel = jnp.where(jnp.logical_or(sel1, sel2), 1.0, 0.0)
    r = lax.broadcasted_iota(I32, (tm, tm), 0)
    c = lax.broadcasted_iota(I32, (tm, tm), 1)
    before = jnp.where(r < c, 1.0, 0.0).astype(BF16)
    base = carry_ref[...][:, 0:1]
    rank = _dot(sel.astype(BF16), before) + base
    r1 = jnp.sum(jnp.where(sel1, rank, 0.0), axis=0, keepdims=True)
    r2 = jnp.sum(jnp.where(sel2, rank, 0.0), axis=0, keepdims=True)
    idx_ref[...] = jnp.concatenate([i1, i2], axis=0)
    gw_ref[...] = jnp.concatenate([w1, w2], axis=0)
    rank_ref[...] = jnp.concatenate([r1, r2], axis=0).astype(I32)
    total = carry_ref[...] + jnp.sum(sel, axis=1, keepdims=True)
    carry_ref[...] = total
    cnt_ref[...] = total.astype(I32)


def _moe_route(x, router, tm):
    t, d = x.shape
    ne = router.shape[1]
    rh, rl = _split2(router.T)
    fixed = lambda i: (0, 0)
    col = lambda i: (0, i)
    return pl.pallas_call(
        _route_kernel,
        grid=(t // tm,),
        in_specs=[pl.BlockSpec((tm, d), lambda i: (i, 0)),
                  pl.BlockSpec((ne, d), fixed), pl.BlockSpec((ne, d), fixed)],
        out_specs=[pl.BlockSpec((TOP_K, tm), col), pl.BlockSpec((TOP_K, tm), col),
                   pl.BlockSpec((TOP_K, tm), col), pl.BlockSpec((ne, LANES), fixed)],
        out_shape=[jax.ShapeDtypeStruct((TOP_K, t), I32), jax.ShapeDtypeStruct((TOP_K, t), F32),
                   jax.ShapeDtypeStruct((TOP_K, t), I32), jax.ShapeDtypeStruct((ne, LANES), I32)],
        scratch_shapes=[pltpu.VMEM((ne, LANES), F32)],
        compiler_params=_cp(("arbitrary",)),
        name="moe_route_top2",
    )(x, rh, rl)


def _group_plan(idx, rank, counts, tm, n_tiles):
    ne = counts.shape[0]
    t = idx.shape[1]
    tiles_e = (counts + tm - 1) // tm
    tile_end = jnp.cumsum(tiles_e)
    tile_start = tile_end - tiles_e
    start_of = jnp.sum(jnp.where(idx[:, :, None] == jnp.arange(ne, dtype=I32), tile_start, 0), axis=-1)
    slots = (start_of * tm + rank).reshape(-1)
    tile_ids = jnp.arange(n_tiles + 1, dtype=I32)
    tile_expert = jnp.minimum(jnp.searchsorted(tile_end, tile_ids, side="right"), ne - 1).astype(I32)
    tile_expert = jnp.concatenate([tile_expert, tile_end[ne - 1:].astype(I32)])
    spare = TOP_K * t + jnp.arange(tm, dtype=I32)
    dst = jnp.tile(spare, n_tiles).at[slots].set(jnp.arange(TOP_K * t, dtype=I32),
                                                 unique_indices=True, mode="promise_in_bounds")
    src = jnp.where(dst < TOP_K * t, dst % t, 0)
    src_tab = jnp.concatenate([src, jnp.zeros((tm,), I32)]).reshape(n_tiles + 1, 1, tm)
    dst_tab = jnp.concatenate([spare, dst]).reshape(n_tiles + 1, 1, tm)
    return tile_expert, src_tab, dst_tab


def _pick(pref, n):
    return pref if n % pref == 0 else n


def kernel(x, p, ln_mix_g, ln_mix_b, ln_ffn_g, ln_ffn_b, fox_w_in, fox_b_f, fox_w_o, ssd_w_in, ssd_conv_w, ssd_conv_b, ssd_dt_bias, ssd_a_log, ssd_d, ssd_norm_g, ssd_w_out, ffn_w_gate, ffn_w_up, ffn_w_down, moe_router, moe_w_gate, moe_w_up, moe_w_down, ple_w_proj, ple_w_gate):
    bsz, seq, d = x.shape
    depth = p.shape[0]
    t = bsz * seq
    alpha = (2.0 * depth) ** 0.25
    x2 = x.reshape(t, d)
    p2 = p.reshape(depth, t, p.shape[-1])
    tm_big = _pick(1024, t)
    tm = _pick(512, t)
    ts = _pick(512, seq)

    w_in = fox_w_in[0]
    qkv = _qkv_proj(x2, w_in[:, :3 * d].astype(BF16), tm_big)
    cum = _fgate_cumlog(x2, w_in[:, 3 * d:], fox_b_f[0], bsz, seq, ts)
    attn = _fox_attention(qkv, cum, bsz, seq, ts)
    h = _proj_residual_ln(attn, fox_w_o[0].astype(BF16), x2, ln_mix_g[0], ln_mix_b[0], alpha, tm,
                          "fox_out_proj_ln")
    h = _ffn_dense(h, ffn_w_gate[0].astype(BF16), ffn_w_up[0].astype(BF16),
                   ffn_w_down[0].astype(BF16), p2[0], ln_ffn_g[0], ln_ffn_b[0],
                   ple_w_proj[0].astype(BF16), ple_w_gate[0].astype(BF16), alpha, tm, 256)

    w_in = ssd_w_in[0]
    d_inner = ssd_norm_g.shape[1]
    conv_ch = ssd_conv_w.shape[2]
    z = _ssd_inproj(h, w_in[:, :d_inner].astype(BF16), tm_big, _pick(1024, d_inner), "ssd_inproj_z")
    xbc = _ssd_inproj_conv(h, w_in[:, d_inner:d_inner + conv_ch].astype(BF16), ssd_conv_w[0],
                           ssd_conv_b[0], seq, _pick(1024, seq), _pick(1024, conv_ch))
    dt, cs = _ssd_dt(h, w_in[:, d_inner + conv_ch:], ssd_dt_bias[0], ssd_a_log[0], ts)
    nh = ssd_dt_bias.shape[1]
    ymix = _ssd_scan(xbc, z, dt[:, :nh], cs[:, :nh], ssd_d[0], ssd_norm_g[0], bsz, seq, ts)
    h = _proj_residual_ln(ymix, ssd_w_out[0].astype(BF16), h, ln_mix_g[1], ln_mix_b[1], alpha, tm,
                          "ssd_out_proj_ln")
    ne = moe_router.shape[2]
    idx, gate_w, rank, counts = _moe_route(h, moe_router[0], tm)
    n_tiles = (TOP_K * t) // tm + ne
    tile_expert, src_tab, dst_tab = _group_plan(idx, rank, counts[:, 0], tm, n_tiles)
    y_tok = _moe_experts(h, moe_w_gate[0].astype(BF16), moe_w_up[0].astype(BF16),
                         moe_w_down[0].astype(BF16), tile_expert, src_tab, dst_tab,
                         TOP_K * t + tm, tm, 512)
    h = _post_moe(h, y_tok, gate_w.T, p2[1], ln_ffn_g[1], ln_ffn_b[1],
                  ple_w_proj[1].astype(BF16), ple_w_gate[1].astype(BF16), alpha, tm)
    return h.reshape(bsz, seq, d)
```

```python
import functools

import jax
import jax.numpy as jnp
from jax import lax
from jax.experimental import pallas as pl
from jax.experimental.pallas import tpu as pltpu

F32 = jnp.float32
BF16 = jnp.bfloat16
I32 = jnp.int32

LN_EPS = 1e-5
RMS_EPS = 1e-5
LANES = 128
HEAD_DIM = 64
SSD_GROUPS = 8
SSD_STATE = 128
SSD_CONV = 4
SSD_CHUNK = 128
TOP_K = 2
LOG2E = 1.4426950408889634
VMEM_LIMIT = 56 * 1024 * 1024


def _cp(sem, vmem=VMEM_LIMIT):
    return pltpu.CompilerParams(dimension_semantics=sem, vmem_limit_bytes=vmem)


def _dot(a, b):
    return jnp.dot(a, b, preferred_element_type=F32)


def _dot_nt(a, b):
    return lax.dot_general(a, b, (((1,), (1,)), ((), ())), preferred_element_type=F32)


def _dot_tn(a, b):
    return lax.dot_general(a, b, (((0,), (0,)), ((), ())), preferred_element_type=F32)


def _split2(v):
    hi = v.astype(BF16)
    lo = (v - hi.astype(F32)).astype(BF16)
    return hi, lo


def _split3(v):
    hi = v.astype(BF16)
    r = v - hi.astype(F32)
    mid = r.astype(BF16)
    lo = (r - mid.astype(F32)).astype(BF16)
    return hi, mid, lo


def _dot_x2w2(x, wh, wl):
    xh, xl = _split2(x)
    n = wh.shape[1]
    both = _dot(xh, jnp.concatenate([wh, wl], axis=1))
    return both[:, :n] + _dot(xl, wh) + both[:, n:]


def _chunk_cumsum(v, chunk):
    rows, n = v.shape
    r = lax.broadcasted_iota(I32, (chunk, chunk), 0)
    c = lax.broadcasted_iota(I32, (chunk, chunk), 1)
    tri = jnp.where(r >= c, 1.0, 0.0).astype(BF16)
    pieces = jnp.concatenate(_split3(v), axis=1)
    outs = []
    for c0 in range(0, rows, chunk):
        s3 = _dot(tri, pieces[c0:c0 + chunk, :])
        outs.append(s3[:, :n] + s3[:, n:2 * n] + s3[:, 2 * n:])
    return outs[0] if len(outs) == 1 else jnp.concatenate(outs, axis=0)


def _layer_norm(v, g, b):
    mu = jnp.mean(v, axis=-1, keepdims=True)
    vc = v - mu
    var = jnp.mean(vc * vc, axis=-1, keepdims=True)
    return vc * lax.rsqrt(var + LN_EPS) * g + b


def _silu(v):
    h = 0.5 * v
    return h + h * jnp.tanh(h)


def _softplus(v):
    return jnp.maximum(v, 0.0) + jnp.log1p(jnp.exp(-jnp.abs(v)))


def _qkv_kernel(x_ref, w_ref, o_ref, *, scale):
    j = pl.program_id(1)
    acc = _dot(x_ref[...].astype(BF16), w_ref[...])
    acc = acc * jnp.where(j == 0, scale, 1.0).astype(F32)
    for c in range(o_ref.shape[0]):
        o_ref[c] = acc[:, c * LANES:(c + 1) * LANES].astype(BF16)


def _qkv_proj(x, w_qkv, tm):
    t, d = x.shape
    n = w_qkv.shape[1]
    tn = d
    cpb = tn // LANES
    return pl.pallas_call(
        functools.partial(_qkv_kernel, scale=LOG2E * HEAD_DIM ** -0.5),
        grid=(t // tm, n // tn),
        in_specs=[pl.BlockSpec((tm, d), lambda i, j: (i, 0)),
                  pl.BlockSpec((d, tn), lambda i, j: (0, j))],
        out_specs=pl.BlockSpec((cpb, tm, LANES), lambda i, j: (j, i, 0)),
        out_shape=jax.ShapeDtypeStruct((n // LANES, t, LANES), BF16),
        compiler_params=_cp(("parallel", "arbitrary")),
        name="fox_qkv_proj",
    )(x, w_qkv)


def _fgate_kernel(x_ref, wh_ref, wl_ref, b_ref, o_ref, carry_ref):
    @pl.when(pl.program_id(1) == 0)
    def _():
        carry_ref[...] = jnp.zeros_like(carry_ref)

    z = _dot_x2w2(x_ref[...], wh_ref[...], wl_ref[...]) + b_ref[...]
    log_f = jnp.minimum(z, 0.0) - jnp.log1p(jnp.exp(-jnp.abs(z)))
    ts = log_f.shape[0]
    cs = _chunk_cumsum(log_f, ts) + carry_ref[...]
    o_ref[...] = cs
    carry_ref[...] = cs[ts - 1:ts, :]


def _fgate_cumlog(x, w_f, b_f, bsz, seq, ts):
    t, d = x.shape
    nh = w_f.shape[1]
    w_pad = jnp.zeros((d, LANES), F32).at[:, :nh].set(w_f)
    wh, wl = _split2(w_pad)
    b_pad = jnp.zeros((1, LANES), F32).at[0, :nh].set(b_f)
    nsb = seq // ts
    return pl.pallas_call(
        _fgate_kernel,
        grid=(bsz, nsb),
        in_specs=[pl.BlockSpec((ts, d), lambda b, s: (b * nsb + s, 0)),
                  pl.BlockSpec((d, LANES), lambda b, s: (0, 0)),
                  pl.BlockSpec((d, LANES), lambda b, s: (0, 0)),
                  pl.BlockSpec((1, LANES), lambda b, s: (0, 0))],
        out_specs=pl.BlockSpec((ts, LANES), lambda b, s: (b * nsb + s, 0)),
        out_shape=jax.ShapeDtypeStruct((t, LANES), F32),
        scratch_shapes=[pltpu.VMEM((1, LANES), F32)],
        compiler_params=_cp(("parallel", "arbitrary")),
        name="fox_forget_cumlog",
    )(x, wh, wl, b_pad)


def _pieces3(v):
    hi = v.astype(BF16).astype(F32)
    r = v - hi
    mid = r.astype(BF16).astype(F32)
    lo = (r - mid).astype(BF16).astype(F32)
    return hi, mid, lo


AUG_ROWS = 16


def _aug_rows(rows):
    n = rows[0].shape[1]
    pad = [jnp.zeros((AUG_ROWS - len(rows), n), F32)]
    return jnp.concatenate(list(rows) + pad, axis=0).astype(BF16)


def _row_to_lane(base):
    r = lax.broadcasted_iota(I32, (AUG_ROWS, LANES), 0)
    lane = lax.broadcasted_iota(I32, (AUG_ROWS, LANES), 1)
    return jnp.where(lane == base + r, 1.0, 0.0).astype(BF16)


def _fox_kernel(q_ref, k_ref, v_ref, c_ref, o_ref, qa_ref, ka_ref, vt_ref, sa_ref, sb_ref, p_ref,
                m_ref, alpha_ref, acc_ref, *, tq, cb):
    qi = pl.program_id(2)
    seq = k_ref.shape[1]
    bases = (HEAD_DIM, 0)

    def data_lanes(lane, h):
        return (lane < HEAD_DIM) if h == 0 else (lane >= HEAD_DIM)

    lane = lax.broadcasted_iota(I32, (tq, LANES), 1)
    ones = jnp.ones((1, tq), F32)
    r128 = lax.broadcasted_iota(I32, (LANES, LANES), 0)
    c128 = lax.broadcasted_iota(I32, (LANES, LANES), 1)
    eye = jnp.where(r128 == c128, 1.0, 0.0).astype(BF16)

    @pl.when(qi == 0)
    def _():
        for c0 in range(0, seq, tq):
            rows = slice(c0, c0 + tq)
            k = k_ref[0, rows, :]
            v = v_ref[0, rows, :]
            hi, mid, lo = _pieces3(c_ref[0, 0, :, rows] * (-LOG2E))
            for h in range(2):
                k_rows = _aug_rows([ones, ones, ones, hi[h:h + 1], mid[h:h + 1], lo[h:h + 1]])
                k_aug = _dot_tn(k_rows, _row_to_lane(bases[h]))
                v_aug = jnp.where(lane == bases[h], 1.0, 0.0).astype(BF16)
                ka_ref[h, rows, :] = jnp.where(data_lanes(lane, h), k, k_aug.astype(BF16))
                vt_ref[h, :, rows] = _dot_nt(eye, jnp.where(data_lanes(lane, h), v, v_aug)).astype(BF16)

    q = q_ref[0]
    hi, mid, lo = _pieces3(c_ref[0, 0, :, pl.ds(pl.multiple_of(qi * tq, tq), tq)] * LOG2E)
    for h in range(2):
        q_rows = _aug_rows([hi[h:h + 1], mid[h:h + 1], lo[h:h + 1], ones, ones, ones])
        q_aug = _dot_tn(q_rows, _row_to_lane(bases[h]))
        qa_ref[h] = jnp.where(data_lanes(lane, h), q, q_aug.astype(BF16))
    m_ref[...] = jnp.full(m_ref.shape, -jnp.inf, F32)
    acc_ref[...] = jnp.zeros(acc_ref.shape, F32)

    def scores(j, s_ref):
        off = pl.multiple_of(j * tq, tq)
        for h in range(2):
            s_ref[h] = _dot_nt(ka_ref[h, pl.ds(off, tq), :], qa_ref[h])

    def softmax_pv(j, s_ref, diagonal):
        off = pl.multiple_of(j * tq, tq)
        for h in range(2):
            for c0 in range(0, tq, cb):
                cols = slice(c0, c0 + cb)
                nrow = c0 + cb if diagonal else tq
                s = s_ref[h, 0:nrow, cols]
                if diagonal:
                    key = lax.broadcasted_iota(I32, s.shape, 0)
                    qry = c0 + lax.broadcasted_iota(I32, s.shape, 1)
                    s = jnp.where(key <= qry, s, -jnp.inf)
                m_old = m_ref[h, :, cols]
                m_new = jnp.maximum(m_old, jnp.max(s, axis=0, keepdims=True))
                m_ref[h, :, cols] = m_new
                alpha_ref[h, :, cols] = jnp.exp2(m_old - m_new)
                p_ref[h, 0:nrow, cols] = jnp.exp2(s - m_new).astype(BF16)
                if nrow < tq:
                    p_ref[h, nrow:tq, cols] = jnp.zeros((tq - nrow, cb), BF16)
            acc_ref[h] = alpha_ref[h] * acc_ref[h] + _dot(vt_ref[h, :, pl.ds(off, tq)], p_ref[h])

    scores(0, sa_ref)

    def pair(i, carry):
        scores(2 * i + 1, sb_ref)
        softmax_pv(2 * i, sa_ref, False)
        scores(2 * i + 2, sa_ref)
        softmax_pv(2 * i + 1, sb_ref, False)
        return carry

    lax.fori_loop(0, qi // 2, pair, 0)

    @pl.when(qi % 2 == 0)
    def _():
        softmax_pv(qi, sa_ref, True)

    @pl.when(qi % 2 == 1)
    def _():
        scores(qi, sb_ref)
        softmax_pv(qi - 1, sa_ref, False)
        softmax_pv(qi, sb_ref, True)

    a0 = acc_ref[0]
    a1 = acc_ref[1]
    row = lax.broadcasted_iota(I32, a0.shape, 0)
    out_t = jnp.where(row < HEAD_DIM, a0 / a0[bases[0]:bases[0] + 1, :],
                      a1 / a1[bases[1]:bases[1] + 1, :])
    o_ref[...] = _dot_tn(out_t.astype(BF16), eye).astype(o_ref.dtype)


def _fox_attention(qkv, cum, bsz, seq, tq):
    hp = qkv.shape[0] // 3
    t = qkv.shape[1]
    c4 = cum[:, :2 * hp].reshape(bsz, seq, hp, 2).transpose(0, 2, 3, 1)
    nq = seq // tq
    return pl.pallas_call(
        functools.partial(_fox_kernel, tq=tq, cb=LANES),
        grid=(bsz, hp, nq),
        in_specs=[pl.BlockSpec((1, tq, LANES), lambda b, h, i: (h, b * nq + i, 0)),
                  pl.BlockSpec((1, seq, LANES), lambda b, h, i: (hp + h, b, 0)),
                  pl.BlockSpec((1, seq, LANES), lambda b, h, i: (2 * hp + h, b, 0)),
                  pl.BlockSpec((1, 1, 2, seq), lambda b, h, i: (b, h, 0, 0))],
        out_specs=pl.BlockSpec((tq, LANES), lambda b, h, i: (b * nq + i, h)),
        out_shape=jax.ShapeDtypeStruct((t, hp * LANES), BF16),
        scratch_shapes=[pltpu.VMEM((2, tq, LANES), BF16),
                        pltpu.VMEM((2, seq, LANES), BF16),
                        pltpu.VMEM((2, LANES, seq), BF16),
                        pltpu.VMEM((2, tq, tq), F32),
                        pltpu.VMEM((2, tq, tq), F32),
                        pltpu.VMEM((2, tq, tq), BF16),
                        pltpu.VMEM((2, 1, tq), F32),
                        pltpu.VMEM((2, 1, tq), F32),
                        pltpu.VMEM((2, LANES, tq), F32)],
        compiler_params=_cp(("parallel", "parallel", "arbitrary")),
        name="fox_flash_attention",
    )(qkv, qkv, qkv, c4)


def _proj_ln_kernel(a_ref, w_ref, res_ref, g_ref, b_ref, o_ref, *, alpha):
    y = _dot(a_ref[...], w_ref[...])
    o_ref[...] = _layer_norm(alpha * res_ref[...] + y, g_ref[...], b_ref[...])


def _proj_residual_ln(a, w, res, g, b, alpha, tm, name):
    t, k = a.shape
    d = w.shape[1]
    return pl.pallas_call(
        functools.partial(_proj_ln_kernel, alpha=alpha),
        grid=(t // tm,),
        in_specs=[pl.BlockSpec((tm, k), lambda i: (i, 0)),
                  pl.BlockSpec((k, d), lambda i: (0, 0)),
                  pl.BlockSpec((tm, d), lambda i: (i, 0)),
                  pl.BlockSpec((1, d), lambda i: (0, 0)),
                  pl.BlockSpec((1, d), lambda i: (0, 0))],
        out_specs=pl.BlockSpec((tm, d), lambda i: (i, 0)),
        out_shape=jax.ShapeDtypeStruct((t, d), F32),
        compiler_params=_cp(("parallel",)),
        name=name,
    )(a, w, res, g.reshape(1, d), b.reshape(1, d))


def _ple(h, p_ref, wp_ref, wgt_ref):
    gate = jax.nn.sigmoid(_dot(h.astype(BF16), wgt_ref[...]))
    proj = _dot(p_ref[...].astype(BF16), wp_ref[...])
    return h + proj * gate


def _ffn_dense_kernel(x_ref, wg_ref, wu_ref, wd_ref, p_ref, g_ref, b_ref, wp_ref, wgt_ref, o_ref,
                      acc_ref, *, fchunk, alpha):
    x = x_ref[...]
    xb = x.astype(BF16)
    f = wg_ref.shape[-1]
    for n, c0 in enumerate(range(0, f, fchunk)):
        c1 = min(c0 + fchunk, f)
        gate = _dot(xb, wg_ref[:, c0:c1])
        up = _dot(xb, wu_ref[:, c0:c1])
        y = _dot((_silu(gate) * up).astype(BF16), wd_ref[c0:c1, :])
        if n == 0:
            acc_ref[...] = y
        else:
            acc_ref[...] += y
    h = _layer_norm(alpha * x + acc_ref[...], g_ref[...], b_ref[...])
    o_ref[...] = _ple(h, p_ref, wp_ref, wgt_ref)


def _ffn_dense(x, w_gate, w_up, w_down, p, g, b, w_proj, w_gate_ple, alpha, tm, fchunk):
    t, d = x.shape
    f = w_gate.shape[1]
    pd = p.shape[1]
    row = lambda i: (i, 0)
    fixed = lambda i: (0, 0)
    once = pl.Buffered(1)
    return pl.pallas_call(
        functools.partial(_ffn_dense_kernel, fchunk=fchunk, alpha=alpha),
        grid=(t // tm,),
        in_specs=[pl.BlockSpec((tm, d), row),
                  pl.BlockSpec((d, f), fixed, pipeline_mode=once),
                  pl.BlockSpec((d, f), fixed, pipeline_mode=once),
                  pl.BlockSpec((f, d), fixed, pipeline_mode=once),
                  pl.BlockSpec((tm, pd), row),
                  pl.BlockSpec((1, d), fixed), pl.BlockSpec((1, d), fixed),
                  pl.BlockSpec((pd, d), fixed, pipeline_mode=once),
                  pl.BlockSpec((d, d), fixed, pipeline_mode=once)],
        out_specs=pl.BlockSpec((tm, d), row),
        out_shape=jax.ShapeDtypeStruct((t, d), F32),
        scratch_shapes=[pltpu.VMEM((tm, d), F32)],
        compiler_params=_cp(("parallel",)),
        name="ffn_swiglu_dense_post",
    )(x, w_gate, w_up, w_down, p, g.reshape(1, d), b.reshape(1, d), w_proj, w_gate_ple)


def _row_copy(src_hbm, row, dst, r, sem):
    return pltpu.make_async_copy(src_hbm.at[pl.ds(row, 1), :], dst.at[pl.ds(r, 1), :], sem)


def _post_moe_kernel(res_ref, y1_ref, y2_ref, gw_ref, p_ref, g_ref, b_ref, wp_ref, wgt_ref,
                     o_ref, *, alpha):
    gw = gw_ref[...]
    ffn = gw[:, 0:1] * y1_ref[...] + gw[:, 1:2] * y2_ref[...]
    h = _layer_norm(alpha * res_ref[...] + ffn, g_ref[...], b_ref[...])
    o_ref[...] = _ple(h, p_ref, wp_ref, wgt_ref)


def _post_moe(res, y_tok, gate_w, p, g, b, w_proj, w_gate, alpha, tm):
    t, d = res.shape
    pd = p.shape[1]
    nt = t // tm
    row = lambda i: (i, 0)
    fixed = lambda i: (0, 0)
    return pl.pallas_call(
        functools.partial(_post_moe_kernel, alpha=alpha),
        grid=(nt,),
        in_specs=[pl.BlockSpec((tm, d), row),
                  pl.BlockSpec((tm, d), row),
                  pl.BlockSpec((tm, d), lambda i: (nt + i, 0)),
                  pl.BlockSpec((tm, TOP_K), row),
                  pl.BlockSpec((tm, pd), row),
                  pl.BlockSpec((1, d), fixed), pl.BlockSpec((1, d), fixed),
                  pl.BlockSpec((pd, d), fixed), pl.BlockSpec((d, d), fixed)],
        out_specs=pl.BlockSpec((tm, d), row),
        out_shape=jax.ShapeDtypeStruct((t, d), F32),
        compiler_params=_cp(("parallel",)),
        name="post_ffn_moe_combine",
    )(res, y_tok, y_tok, gate_w, p, g.reshape(1, d), b.reshape(1, d), w_proj, w_gate)


def _moe_experts_kernel(te_ref, src_ref, nxt_ref, dst_ref, x_hbm, wg_ref, wu_ref, wd_ref, y_hbm,
                        xbuf0, xbuf1, ybuf0, ybuf1, xb_ref, gsem, ssem, *, fchunk):
    i = pl.program_id(0)
    last = te_ref[pl.num_programs(0)]
    tm = xbuf0.shape[0]
    xbufs = (xbuf0, xbuf1)
    ybufs = (ybuf0, ybuf1)

    def gather(row, buf, r, s):
        return _row_copy(x_hbm, row, buf, r, gsem.at[s])

    def scatter(buf, r, row, s):
        return pltpu.make_async_copy(buf.at[pl.ds(r, 1), :], y_hbm.at[pl.ds(row, 1), :], ssem.at[s])

    def wait_all(make):
        def body(r, _):
            make(r).wait()
            return 0
        lax.fori_loop(0, tm, body, 0, unroll=8)

    @pl.when(i == 0)
    def _():
        def body(r, _):
            gather(src_ref[0, 0, r], xbuf0, r, 0).start()
            return 0
        lax.fori_loop(0, tm, body, 0, unroll=8)
        ybuf1[...] = jnp.zeros_like(ybuf1)

    def step(s):
        o = 1 - s
        wait_all(lambda r: gather(0, xbufs[s], r, s))

        @pl.when(i >= 1)
        def _():
            wait_all(lambda r: scatter(ybufs[s], r, 0, s))

        xb_ref[...] = xbufs[s][...].astype(BF16)
        f = wg_ref.shape[-1]
        starts = list(range(0, f, fchunk))
        per = -(-tm // len(starts))
        for n, c0 in enumerate(starts):
            for r in range(n * per, min((n + 1) * per, tm)):
                gather(nxt_ref[0, 0, r], xbufs[o], r, o).start()
                scatter(ybufs[o], r, dst_ref[0, 0, r], o).start()
            c1 = min(c0 + fchunk, f)
            g = _dot(xb_ref[...], wg_ref[0, :, c0:c1])
            u = _dot(xb_ref[...], wu_ref[0, :, c0:c1])
            y = _dot((_silu(g) * u).astype(BF16), wd_ref[0, c0:c1, :])
            if n == 0:
                ybufs[s][...] = y
            else:
                ybufs[s][...] += y

        @pl.when(i == last)
        def _():
            wait_all(lambda r: gather(0, xbufs[o], r, o))
            wait_all(lambda r: scatter(ybufs[o], r, 0, o))

    for s in range(2):
        pl.when(jnp.logical_and(i % 2 == s, i <= last))(functools.partial(step, s))


def _moe_experts(x, w_gate, w_up, w_down, tile_expert, src_tab, dst_tab, n_out_rows, tm, fchunk):
    d = x.shape[1]
    f = w_gate.shape[-1]
    steps = tile_expert.shape[0] - 1
    once = pl.Buffered(1)
    smem_tile = lambda index_map: pl.BlockSpec((1, 1, tm), index_map, memory_space=pltpu.SMEM)
    grid_spec = pltpu.PrefetchScalarGridSpec(
        num_scalar_prefetch=1,
        grid=(steps,),
        in_specs=[smem_tile(lambda i, te: (i, 0, 0)),
                  smem_tile(lambda i, te: (jnp.minimum(i + 1, steps - 1), 0, 0)),
                  smem_tile(lambda i, te: (i, 0, 0)),
                  pl.BlockSpec(memory_space=pl.ANY),
                  pl.BlockSpec((1, d, f), lambda i, te: (te[i], 0, 0), pipeline_mode=once),
                  pl.BlockSpec((1, d, f), lambda i, te: (te[i], 0, 0), pipeline_mode=once),
                  pl.BlockSpec((1, f, d), lambda i, te: (te[i], 0, 0), pipeline_mode=once)],
        out_specs=pl.BlockSpec(memory_space=pl.ANY),
        scratch_shapes=[pltpu.VMEM((tm, d), F32), pltpu.VMEM((tm, d), F32),
                        pltpu.VMEM((tm, d), F32), pltpu.VMEM((tm, d), F32),
                        pltpu.VMEM((tm, d), BF16),
                        pltpu.SemaphoreType.DMA((2,)), pltpu.SemaphoreType.DMA((2,))],
    )
    return pl.pallas_call(
        functools.partial(_moe_experts_kernel, fchunk=fchunk),
        grid_spec=grid_spec,
        out_shape=jax.ShapeDtypeStruct((n_out_rows, d), F32),
        compiler_params=_cp(("arbitrary",)),
        name="moe_experts_fused_dispatch",
    )(tile_expert, src_tab, src_tab, dst_tab, x, w_gate, w_up, w_down)


def _inproj_kernel(x_ref, w_ref, o_ref):
    o_ref[...] = _dot(x_ref[...].astype(BF16), w_ref[...]).astype(o_ref.dtype)


def _inproj_conv_kernel(x_ref, w_ref, cw_ref, cb_ref, o_ref, ext_ref, carry_ref, *, tiles_per_seq):
    i = pl.program_id(0)
    j = pl.program_id(1)
    tm = x_ref.shape[0]
    halo = carry_ref.shape[1]

    @pl.when(i % tiles_per_seq == 0)
    def _():
        carry_ref[j] = jnp.zeros(carry_ref.shape[1:], F32)

    acc = _dot(x_ref[...].astype(BF16), w_ref[...])
    ext_ref[0:halo, :] = carry_ref[j]
    ext_ref[halo:halo + tm, :] = acc
    cw = cw_ref[...]
    out = cb_ref[...] + cw[0:1, :] * ext_ref[halo - 3:halo - 3 + tm, :]
    for k in range(1, SSD_CONV):
        out = out + cw[k:k + 1, :] * ext_ref[halo - 3 + k:halo - 3 + k + tm, :]
    carry_ref[j] = acc[tm - halo:tm, :]
    o_ref[...] = _silu(out).astype(o_ref.dtype)


def _ssd_inproj(x, w, tm, tn, name):
    t, d = x.shape
    n = w.shape[1]
    return pl.pallas_call(
        _inproj_kernel,
        grid=(t // tm, n // tn),
        in_specs=[pl.BlockSpec((tm, d), lambda i, j: (i, 0)),
                  pl.BlockSpec((d, tn), lambda i, j: (0, j))],
        out_specs=pl.BlockSpec((tm, tn), lambda i, j: (i, j)),
        out_shape=jax.ShapeDtypeStruct((t, n), BF16),
        compiler_params=_cp(("parallel", "arbitrary")),
        name=name,
    )(x, w)


def _ssd_inproj_conv(x, w, conv_w, conv_b, seq, tm, tn):
    t, d = x.shape
    n = w.shape[1]
    halo = 8
    return pl.pallas_call(
        functools.partial(_inproj_conv_kernel, tiles_per_seq=seq // tm),
        grid=(t // tm, n // tn),
        in_specs=[pl.BlockSpec((tm, d), lambda i, j: (i, 0)),
                  pl.BlockSpec((d, tn), lambda i, j: (0, j)),
                  pl.BlockSpec((SSD_CONV, tn), lambda i, j: (0, j)),
                  pl.BlockSpec((1, tn), lambda i, j: (0, j))],
        out_specs=pl.BlockSpec((tm, tn), lambda i, j: (i, j)),
        out_shape=jax.ShapeDtypeStruct((t, n), BF16),
        scratch_shapes=[pltpu.VMEM((tm + halo, tn), F32), pltpu.VMEM((n // tn, halo, tn), F32)],
        compiler_params=_cp(("arbitrary", "arbitrary")),
        name="ssd_inproj_conv_silu",
    )(x, w, conv_w, conv_b.reshape(1, n))


def _ssd_dt_kernel(x_ref, wh_ref, wl_ref, bias_ref, alog_ref, dt_ref, cs_ref):
    dt = _softplus(_dot_x2w2(x_ref[...], wh_ref[...], wl_ref[...]) + bias_ref[...])
    da = dt * (-jnp.exp(alog_ref[...]))
    dt_ref[...] = dt
    cs_ref[...] = _chunk_cumsum(da, SSD_CHUNK)


def _ssd_dt(x, w_dt, dt_bias, a_log, ts):
    t, d = x.shape
    nh = w_dt.shape[1]
    w_pad = jnp.zeros((d, LANES), F32).at[:, :nh].set(w_dt)
    wh, wl = _split2(w_pad)
    bias = jnp.zeros((1, LANES), F32).at[0, :nh].set(dt_bias)
    alog = jnp.zeros((1, LANES), F32).at[0, :nh].set(a_log)
    fixed = lambda i: (0, 0)
    row = lambda i: (i, 0)
    return pl.pallas_call(
        _ssd_dt_kernel,
        grid=(t // ts,),
        in_specs=[pl.BlockSpec((ts, d), row), pl.BlockSpec((d, LANES), fixed),
                  pl.BlockSpec((d, LANES), fixed), pl.BlockSpec((1, LANES), fixed),
                  pl.BlockSpec((1, LANES), fixed)],
        out_specs=[pl.BlockSpec((ts, LANES), row), pl.BlockSpec((ts, LANES), row)],
        out_shape=[jax.ShapeDtypeStruct((t, LANES), F32), jax.ShapeDtypeStruct((t, LANES), F32)],
        compiler_params=_cp(("parallel",)),
        name="ssd_dt_prep",
    )(x, wh, wl, bias, alog)


def _ssd_scan_kernel(xs_ref, bm_ref, cm_ref, z_ref, dtt_ref, cst_ref, d_ref, ng_ref,
                     o_ref, h_ref):
    @pl.when(pl.program_id(2) == 0)
    def _():
        h_ref[...] = jnp.zeros_like(h_ref)

    lb, gw = xs_ref.shape
    hpg = gw // HEAD_DIM
    cl = SSD_CHUNK
    lane_head = lax.broadcasted_iota(I32, (cl, gw), 1) // HEAD_DIM
    r = lax.broadcasted_iota(I32, (cl, cl), 0)
    c = lax.broadcasted_iota(I32, (cl, cl), 1)
    causal = r >= c

    def spread(width, lanes_per_head):
        row = lax.broadcasted_iota(I32, (AUG_ROWS, width), 0)
        lane = lax.broadcasted_iota(I32, (AUG_ROWS, width), 1)
        head = row - (row // hpg) * hpg
        hit = jnp.logical_and(row < 3 * hpg, head == lane // lanes_per_head)
        return jnp.where(hit, 1.0, 0.0).astype(BF16)

    to_group = spread(gw, HEAD_DIM)
    to_blocks = spread(hpg * cl, cl)

    def piece_rows(v):
        pad = jnp.zeros((AUG_ROWS - 3 * hpg, v.shape[1]), F32)
        return jnp.concatenate(list(_pieces3(v)) + [pad], axis=0).astype(BF16)

    h = h_ref[...]
    for ci in range(lb // cl):
        sl = slice(ci * cl, (ci + 1) * cl)
        x = xs_ref[sl, :].astype(F32)
        bm = bm_ref[sl, :]
        cm = cm_ref[sl, :]
        cst = cst_ref[0, :, sl]
        cs_rows = piece_rows(cst)
        cs_e = _dot_tn(cs_rows, to_group)
        cs_b = _dot_tn(cs_rows, to_blocks)
        end_e = cs_e[cl - 1:cl, :]
        xdt = x * _dot_tn(piece_rows(dtt_ref[0, :, sl]), to_group)
        cb = _dot_nt(cm, bm)
        ms = []
        for hh in range(hpg):
            seg = cs_b[:, hh * cl:(hh + 1) * cl] - cst[hh:hh + 1, :]
            decay = jnp.exp(jnp.where(causal, seg, -jnp.inf))
            ms.append((cb * decay).astype(BF16))
        y_all = _dot(jnp.concatenate(ms, axis=0), xdt.astype(BF16))
        y = y_all[(hpg - 1) * cl:hpg * cl, :]
        for hh in range(hpg - 2, -1, -1):
            y = jnp.where(lane_head == hh, y_all[hh * cl:(hh + 1) * cl, :], y)
        y = y + _dot(cm, h.astype(BF16)) * jnp.exp(cs_e)
        y = y + d_ref[0] * x
        states = _dot_tn(bm, (xdt * jnp.exp(end_e - cs_e)).astype(BF16))
        h = jnp.exp(end_e) * h + states
        y = y * _silu(z_ref[sl, :].astype(F32))
        y = y * lax.rsqrt(jnp.mean(y * y, axis=-1, keepdims=True) + RMS_EPS)
        o_ref[sl, :] = (y * ng_ref[...]).astype(o_ref.dtype)
    h_ref[...] = h


def _ssd_scan(xbc, z, dt, cs, d_skip, norm_g, bsz, seq, lb):
    t = xbc.shape[0]
    d_inner = z.shape[1]
    g = SSD_GROUPS
    gw = d_inner // g
    hpg = gw // HEAD_DIM
    n = SSD_STATE
    nsb = seq // lb
    assert SSD_CHUNK == LANES and 3 * hpg <= AUG_ROWS
    dt_t = dt.reshape(t, g, hpg).transpose(1, 2, 0)
    cs_t = cs.reshape(t, g, hpg).transpose(1, 2, 0)
    d_e =jnp.repeat(d_skip.astype(F32), HEAD_DIM).reshape(g, 1, gw)
    rows = lambda b, gi, s: b * nsb + s
    return pl.pallas_call(
        _ssd_scan_kernel,
        grid=(bsz, g, nsb),
        in_specs=[pl.BlockSpec((lb, gw), lambda b, gi, s: (rows(b, gi, s), gi)),
                  pl.BlockSpec((lb, n), lambda b, gi, s: (rows(b, gi, s), d_inner // n + gi)),
                  pl.BlockSpec((lb, n), lambda b, gi, s: (rows(b, gi, s), d_inner // n + g + gi)),
                  pl.BlockSpec((lb, gw), lambda b, gi, s: (rows(b, gi, s), gi)),
                  pl.BlockSpec((1, hpg, lb), lambda b, gi, s: (gi, 0, rows(b, gi, s))),
                  pl.BlockSpec((1, hpg, lb), lambda b, gi, s: (gi, 0, rows(b, gi, s))),
                  pl.BlockSpec((1, 1, gw), lambda b, gi, s: (gi, 0, 0)),
                  pl.BlockSpec((1, gw), lambda b, gi, s: (0, gi))],
        out_specs=pl.BlockSpec((lb, gw), lambda b, gi, s: (rows(b, gi, s), gi)),
        out_shape=jax.ShapeDtypeStruct((t, d_inner), BF16),
        scratch_shapes=[pltpu.VMEM((n, gw), F32)],
        compiler_params=_cp(("parallel", "parallel", "arbitrary")),
        name="ssd_chunk_scan",
    )(xbc, xbc, xbc, z, dt_t, cs_t, d_e, norm_g.reshape(1, d_inner))


def _route_kernel(x_ref, rh_ref, rl_ref, idx_ref, gw_ref, rank_ref, cnt_ref, carry_ref):
    @pl.when(pl.program_id(0) == 0)
    def _():
        carry_ref[...] = jnp.zeros_like(carry_ref)

    xh, xl = _split2(x_ref[...])
    rh = rh_ref[...]
    logits = _dot_nt(rh, xh) + _dot_nt(rh, xl) + _dot_nt(rl_ref[...], xh)
    ne, tm = logits.shape
    e = lax.broadcasted_iota(I32, (ne, tm), 0)
    m1 = jnp.max(logits, axis=0, keepdims=True)
    i1 = jnp.min(jnp.where(logits == m1, e, ne), axis=0, keepdims=True)
    rest = jnp.where(e == i1, -jnp.inf, logits)
    m2 = jnp.max(rest, axis=0, keepdims=True)
    i2 = jnp.min(jnp.where(rest == m2, e, ne), axis=0, keepdims=True)
    ex = jnp.exp(m2 - m1)
    w1 = 1.0 / (1.0 + ex)
    w2 = ex / (1.0 + ex)
    sel1 = e == i1
    sel2 = e == i2
    sel = jnp.where(jnp.logical_or(sel1, sel2), 1.0, 0.0)
    r = lax.broadcasted_iota(I32, (tm, tm), 0)
    c = lax.broadcasted_iota(I32, (tm, tm), 1)
    before = jnp.where(r < c, 1.0, 0.0).astype(BF16)
    base = carry_ref[...][:, 0:1]
    rank = _dot(sel.astype(BF16), before) + base
    r1 = jnp.sum(jnp.where(sel1, rank, 0.0), axis=0, keepdims=True)
    r2 = jnp.sum(jnp.where(sel2, rank, 0.0), axis=0, keepdims=True)
    idx_ref[...] = jnp.concatenate([i1, i2], axis=0)
    gw_ref[...] = jnp.concatenate([w1, w2], axis=0)
    rank_ref[...] = jnp.concatenate([r1, r2], axis=0).astype(I32)
    total = carry_ref[...] + jnp.sum(sel, axis=1, keepdims=True)
    carry_ref[...] = total
    cnt_ref[...] = total.astype(I32)


def _moe_route(x, router, tm):
    t, d = x.shape
    ne = router.shape[1]
    rh, rl = _split2(router.T)
    fixed = lambda i: (0, 0)
    col = lambda i: (0, i)
    return pl.pallas_call(
        _route_kernel,
        grid=(t // tm,),
        in_specs=[pl.BlockSpec((tm, d), lambda i: (i, 0)),
                  pl.BlockSpec((ne, d), fixed), pl.BlockSpec((ne, d), fixed)],
        out_specs=[pl.BlockSpec((TOP_K, tm), col), pl.BlockSpec((TOP_K, tm), col),
                   pl.BlockSpec((TOP_K, tm), col), pl.BlockSpec((ne, LANES), fixed)],
        out_shape=[jax.ShapeDtypeStruct((TOP_K, t), I32), jax.ShapeDtypeStruct((TOP_K, t), F32),
                   jax.ShapeDtypeStruct((TOP_K, t), I32), jax.ShapeDtypeStruct((ne, LANES), I32)],
        scratch_shapes=[pltpu.VMEM((ne, LANES), F32)],
        compiler_params=_cp(("arbitrary",)),
        name="moe_route_top2",
    )(x, rh, rl)


def _group_plan(idx, rank, counts, tm, n_tiles):
    ne = counts.shape[0]
    t = idx.shape[1]
    tiles_e = (counts + tm - 1) // tm
    tile_end = jnp.cumsum(tiles_e)
    tile_start = tile_end - tiles_e
    start_of = jnp.sum(jnp.where(idx[:, :, None] == jnp.arange(ne, dtype=I32), tile_start, 0), axis=-1)
    slots = (start_of * tm + rank).reshape(-1)
    tile_ids = jnp.arange(n_tiles + 1, dtype=I32)
    tile_expert = jnp.minimum(jnp.searchsorted(tile_end, tile_ids, side="right"), ne - 1).astype(I32)
    tile_expert = jnp.concatenate([tile_expert, tile_end[ne - 1:].astype(I32)])
    spare = TOP_K * t + jnp.arange(tm, dtype=I32)
    dst = jnp.tile(spare, n_tiles).at[slots].set(jnp.arange(TOP_K * t, dtype=I32),
                                                 unique_indices=True, mode="promise_in_bounds")
    src = jnp.where(dst < TOP_K * t, dst % t, 0)
    src_tab = jnp.concatenate([src, jnp.zeros((tm,), I32)]).reshape(n_tiles + 1, 1, tm)
    dst_tab = jnp.concatenate([spare, dst]).reshape(n_tiles + 1, 1, tm)
    return tile_expert, src_tab, dst_tab


def _pick(pref, n):
    return pref if n % pref == 0 else n


def kernel(x, p, ln_mix_g, ln_mix_b, ln_ffn_g, ln_ffn_b, fox_w_in, fox_b_f, fox_w_o, ssd_w_in, ssd_conv_w, ssd_conv_b, ssd_dt_bias, ssd_a_log, ssd_d, ssd_norm_g, ssd_w_out, ffn_w_gate, ffn_w_up, ffn_w_down, moe_router, moe_w_gate, moe_w_up, moe_w_down, ple_w_proj, ple_w_gate):
    bsz, seq, d = x.shape
    depth = p.shape[0]
    t = bsz * seq
    alpha = (2.0 * depth) ** 0.25
    x2 = x.reshape(t, d)
    p2 = p.reshape(depth, t, p.shape[-1])
    tm_big = _pick(1024, t)
    tm = _pick(512, t)
    ts = _pick(512, seq)

    w_in = fox_w_in[0]
    qkv = _qkv_proj(x2, w_in[:, :3 * d].astype(BF16), tm_big)
    cum = _fgate_cumlog(x2, w_in[:, 3 * d:], fox_b_f[0], bsz, seq, ts)
    attn = _fox_attention(qkv, cum, bsz, seq, ts)
    h = _proj_residual_ln(attn, fox_w_o[0].astype(BF16), x2, ln_mix_g[0], ln_mix_b[0], alpha, tm,
                          "fox_out_proj_ln")
    h = _ffn_dense(h, ffn_w_gate[0].astype(BF16), ffn_w_up[0].astype(BF16),
                   ffn_w_down[0].astype(BF16), p2[0], ln_ffn_g[0], ln_ffn_b[0],
                   ple_w_proj[0].astype(BF16), ple_w_gate[0].astype(BF16), alpha, tm, 256)

    w_in = ssd_w_in[0]
    d_inner = ssd_norm_g.shape[1]
    conv_ch = ssd_conv_w.shape[2]
    z = _ssd_inproj(h, w_in[:, :d_inner].astype(BF16), tm_big, _pick(1024, d_inner), "ssd_inproj_z")
    xbc = _ssd_inproj_conv(h, w_in[:, d_inner:d_inner + conv_ch].astype(BF16), ssd_conv_w[0],
                           ssd_conv_b[0], seq, _pick(1024, seq), _pick(1024, conv_ch))
    dt, cs = _ssd_dt(h, w_in[:, d_inner + conv_ch:], ssd_dt_bias[0], ssd_a_log[0], ts)
    nh = ssd_dt_bias.shape[1]
    ymix = _ssd_scan(xbc, z, dt[:, :nh], cs[:, :nh], ssd_d[0], ssd_norm_g[0], bsz, seq, ts)
    h = _proj_residual_ln(ymix, ssd_w_out[0].astype(BF16), h, ln_mix_g[1], ln_mix_b[1], alpha, tm,
                          "ssd_out_proj_ln")
    ne = moe_router.shape[2]
    idx, gate_w, rank, counts = _moe_route(h, moe_router[0], tm)
    n_tiles = (TOP_K * t) // tm + ne
    tile_expert, src_tab, dst_tab = _group_plan(idx, rank, counts[:, 0], tm, n_tiles)
    y_tok = _moe_experts(h, moe_w_gate[0].astype(BF16), moe_w_up[0].astype(BF16),
                         moe_w_down[0].astype(BF16), tile_expert, src_tab, dst_tab,
                         TOP_K * t + tm, tm, 1792)
    h = _post_moe(h, y_tok, gate_w.T, p2[1], ln_ffn_g[1], ln_ffn_b[1],
                  ple_w_proj[1].astype(BF16), ple_w_gate[1].astype(BF16), alpha, tm)
    return h.reshape(bsz, seq, d)
```
